```python
import jax, jax.numpy as jnp
from jax import lax
import numpy as np

D_MODEL = 1024
BATCH = 8
SEQ = 2048
DEPTH = 4
DEC_BATCH = 128
DEC_SEQ = 8
PAST_LEN = 16384
PAGE_SIZE = 128

N_MIXERS = 2
EXPAND = 2
D_INNER = EXPAND * D_MODEL
HEAD_DIM = 64
N_HEADS = D_INNER // HEAD_DIM
W_LORA = 96
A_LORA = 96
V_LORA = 64
POOL_WINDOWS = (2, 4, 8, 16)
N_POOL_GROUPS = len(POOL_WINDOWS)
POOL_GROUP = D_INNER // N_POOL_GROUPS
POOL_BUF = max(POOL_WINDOWS) - 1
N_RWKV = (DEPTH + 1) // 2
N_POOL = DEPTH // 2
NORM_EPS = 1e-6
GN_EPS = 64e-5

kernel_name = 'rwkv7_pool_hybrid_step'


def rms_norm(x, g):
    xf = x.astype(jnp.float32)
    y = xf * lax.rsqrt(jnp.mean(jnp.square(xf), axis=-1, keepdims=True) + NORM_EPS)
    return y.astype(x.dtype) * g


def wkv7_scan(S0, r, w, k, v, a, b):
    def step(S, inp):
        r_t, w_t, k_t, v_t, a_t, b_t = inp
        Sa = jnp.einsum('bhij,bhj->bhi', S, a_t)
        S = S * w_t[:, :, None, :] + Sa[..., None] * b_t[:, :, None, :] + v_t[..., None] * k_t[:, :, None, :]
        return S, jnp.einsum('bhij,bhj->bhi', S, r_t)
    xs = (jnp.moveaxis(r, 1, 0), jnp.moveaxis(w, 1, 0), jnp.moveaxis(k, 1, 0),
          jnp.moveaxis(v, 1, 0), jnp.moveaxis(a, 1, 0), jnp.moveaxis(b, 1, 0))
    S, ys = lax.scan(step, S0, xs)
    return jnp.moveaxis(ys, 0, 1), S


def rwkv7_branch(xn, shift_prev, wkv_prev, v_first, vres, mu, w_r, w_k, w_v, w_z,
                 w0, w1, w2, a0, a1, a2, k_k, k_a, r_k, gn_g, gn_b, w_o):
    B, T, _ = xn.shape
    f32 = jnp.float32
    x_prev = jnp.concatenate([shift_prev[:, None, :].astype(xn.dtype), xn[:, :-1]], axis=1)
    dx = x_prev - xn
    xr, xw, xk, xv, xa, xg = (xn + dx * mu[m] for m in range(6))
    r = xr @ w_r
    k = xk @ w_k
    v = xv @ w_v
    z = xg @ w_z
    w_log = -jax.nn.softplus(-(w0 + jnp.tanh(xw @ w1) @ w2).astype(f32)) - 0.5
    decay = jnp.exp(-jnp.exp(w_log))
    a = jax.nn.sigmoid((a0 + (xa @ a1) @ a2).astype(f32))
    if vres is None:
        v_first = v
    else:
        v0, v1, v2 = vres
        v = v + (v_first - v) * jax.nn.sigmoid(v0 + (xv @ v1) @ v2)
    shp = (B, T, N_HEADS, HEAD_DIM)
    r_h = r.astype(f32).reshape(shp)
    v_h = v.astype(f32).reshape(shp)
    a_h = a.reshape(shp)
    w_h = decay.reshape(shp)
    kk = (k * k_k).astype(f32).reshape(shp)
    kk = kk / jnp.maximum(jnp.sqrt(jnp.sum(jnp.square(kk), axis=-1, keepdims=True)), 1e-12)
    k_h = k.astype(f32).reshape(shp) * (1.0 + (a_h - 1.0) * k_a.astype(f32).reshape(N_HEADS, HEAD_DIM))
    y, S = wkv7_scan(wkv_prev.astype(f32), r_h, w_h, k_h, v_h, -kk, kk * a_h)
    mean = jnp.mean(y, axis=-1, keepdims=True)
    var = jnp.mean(jnp.square(y - mean), axis=-1, keepdims=True)
    y = ((y - mean) * lax.rsqrt(var + GN_EPS)).reshape(B, T, D_INNER) * gn_g + gn_b
    bonus = jnp.sum(r_h * k_h * r_k.astype(f32), axis=-1, keepdims=True) * v_h
    y = y + bonus.reshape(B, T, D_INNER)
    out = (y.astype(xn.dtype) * jax.nn.silu(z)) @ w_o
    return out, v_first, xn[:, -1], S.astype(wkv_prev.dtype)


def pool_branch(xn, buf_prev, pos0, w_in, w_grp, b_grp, scale, w_o):
    B, T, _ = xn.shape
    f32 = jnp.float32
    uz = xn @ w_in
    u, z = uz[..., :D_INNER], uz[..., D_INNER:]
    ext = jnp.concatenate([buf_prev.astype(u.dtype), u], axis=1)
    cs = jnp.cumsum(ext.astype(f32), axis=1)
    cs = jnp.concatenate([jnp.zeros((B, 1, D_INNER), f32), cs], axis=1)
    end = cs[:, POOL_BUF + 1:]
    pos = pos0 + jnp.arange(T, dtype=jnp.int32)
    pooled = []
    for g, win in enumerate(POOL_WINDOWS):
        lo, hi = g * POOL_GROUP, (g + 1) * POOL_GROUP
        start = cs[:, POOL_BUF + 1 - win:POOL_BUF + 1 - win + T, lo:hi]
        cnt = jnp.minimum(win, pos + 1).astype(f32)[None, :, None]
        pooled.append((end[..., lo:hi] - start) / cnt)
    p = (jnp.concatenate(pooled, axis=-1) - u.astype(f32)).astype(u.dtype)
    p = p.reshape(B, T, N_POOL_GROUPS, POOL_GROUP)
    mixed = jnp.einsum('btgi,gio->btgo', p, w_grp) + b_grp
    mixed = mixed.reshape(B, T, D_INNER) * scale
    out = (mixed * jax.nn.silu(z)) @ w_o
    return out, ext[:, -POOL_BUF:]


def trunk(x, shift0, wkv0, buf0, pos0, p):
    new_shift, new_wkv, new_buf = [], [], []
    v_first = None
    for i in range(DEPTH):
        j = i // N_MIXERS
        if i % N_MIXERS == 0:
            xn = rms_norm(x, p['rwkv_norm'][j])
            vres = None if j == 0 else (p['rwkv_v0'][j - 1], p['rwkv_v1'][j - 1], p['rwkv_v2'][j - 1])
            out, v_first, sh, S = rwkv7_branch(
                xn, shift0[j], wkv0[j], v_first, vres, p['rwkv_mu'][j],
                p['rwkv_w_r'][j], p['rwkv_w_k'][j], p['rwkv_w_v'][j], p['rwkv_w_z'][j],
                p['rwkv_w0'][j], p['rwkv_w1'][j], p['rwkv_w2'][j],
                p['rwkv_a0'][j], p['rwkv_a1'][j], p['rwkv_a2'][j],
                p['rwkv_k_k'][j], p['rwkv_k_a'][j], p['rwkv_r_k'][j],
                p['rwkv_gn_g'][j], p['rwkv_gn_b'][j], p['rwkv_w_o'][j])
            new_shift.append(sh)
            new_wkv.append(S)
        else:
            xn = rms_norm(x, p['pool_norm'][j])
            out, buf = pool_branch(xn, buf0[j], pos0, p['pool_w_in'][j], p['pool_w_grp'][j],
                                   p['pool_b_grp'][j], p['pool_scale'][j], p['pool_w_o'][j])
            new_buf.append(buf)
        x = x + out
    return rms_norm(x, p['final_norm']), jnp.stack(new_shift), jnp.stack(new_wkv), jnp.stack(new_buf)


def setup_inputs(seed: int = 0) -> dict:
    key = jax.random.key(seed)
    ks = iter(jax.random.split(key, 64))
    f32 = jnp.float32
    def nrm(shape, s):
        return jax.random.normal(next(ks), shape, f32) * s
    def uni(shape, lo, hi):
        return jax.random.uniform(next(ks), shape, f32, lo, hi)
    D, C, NR, NP = D_MODEL, D_INNER, N_RWKV, N_POOL
    NV = max(NR - 1, 0)
    return {
        'x_prompt': nrm((BATCH, SEQ, D), 1.0),
        'x_sample': nrm((DEC_BATCH, DEC_SEQ, D), 1.0),
        'state_shift': nrm((NR, DEC_BATCH, D), 1.0),
        'state_wkv': nrm((NR, DEC_BATCH, N_HEADS, HEAD_DIM, HEAD_DIM), 0.5),
        'state_pool': nrm((NP, DEC_BATCH, POOL_BUF, C), 1.0),
        'rwkv_norm': 1.0 + nrm((NR, D), 0.05),
        'rwkv_mu': uni((NR, 6, D), 0.0, 1.0),
        'rwkv_w_r': nrm((NR, D, C), D ** -0.5),
        'rwkv_w_k': nrm((NR, D, C), D ** -0.5),
        'rwkv_w_v': nrm((NR, D, C), D ** -0.5),
        'rwkv_w_z': nrm((NR, D, C), D ** -0.5),
        'rwkv_w0': uni((NR, C), -6.0, -1.0),
        'rwkv_w1': nrm((NR, D, W_LORA), D ** -0.5),
        'rwkv_w2': nrm((NR, W_LORA, C), 0.3 * W_LORA ** -0.5),
        'rwkv_a0': nrm((NR, C), 0.1),
        'rwkv_a1': nrm((NR, D, A_LORA), D ** -0.5),
        'rwkv_a2': nrm((NR, A_LORA, C), 0.5 * A_LORA ** -0.5),
        'rwkv_v0': nrm((NV, C), 0.1),
        'rwkv_v1': nrm((NV, D, V_LORA), D ** -0.5),
        'rwkv_v2': nrm((NV, V_LORA, C), 0.5 * V_LORA ** -0.5),
        'rwkv_k_k': 0.85 + nrm((NR, C), 0.05),
        'rwkv_k_a': 1.0 + nrm((NR, C), 0.05),
        'rwkv_r_k': nrm((NR, N_HEADS, HEAD_DIM), 0.1),
        'rwkv_gn_g': 1.0 + nrm((NR, C), 0.05),
        'rwkv_gn_b': nrm((NR, C), 0.02),
        'rwkv_w_o': nrm((NR, C, D), C ** -0.5),
        'pool_norm': 1.0 + nrm((NP, D), 0.05),
        'pool_w_in': nrm((NP, D, 2 * C), D ** -0.5),
        'pool_w_grp': nrm((NP, N_POOL_GROUPS, POOL_GROUP, POOL_GROUP), POOL_GROUP ** -0.5),
        'pool_b_grp': nrm((NP, N_POOL_GROUPS, POOL_GROUP), 0.02),
        'pool_scale': 1.0 + nrm((NP, C), 0.1),
        'pool_w_o': nrm((NP, C, D), C ** -0.5),
        'final_norm': 1.0 + nrm((D,), 0.05),
    }


def reference(x_prompt, x_sample, state_shift, state_wkv, state_pool,
              rwkv_norm, rwkv_mu, rwkv_w_r, rwkv_w_k, rwkv_w_v, rwkv_w_z,
              rwkv_w0, rwkv_w1, rwkv_w2, rwkv_a0, rwkv_a1, rwkv_a2,
              rwkv_v0, rwkv_v1, rwkv_v2, rwkv_k_k, rwkv_k_a, rwkv_r_k,
              rwkv_gn_g, rwkv_gn_b, rwkv_w_o,
              pool_norm, pool_w_in, pool_w_grp, pool_b_grp, pool_scale, pool_w_o,
              final_norm):
    p = dict(rwkv_norm=rwkv_norm, rwkv_mu=rwkv_mu, rwkv_w_r=rwkv_w_r, rwkv_w_k=rwkv_w_k,
             rwkv_w_v=rwkv_w_v, rwkv_w_z=rwkv_w_z, rwkv_w0=rwkv_w0, rwkv_w1=rwkv_w1,
             rwkv_w2=rwkv_w2, rwkv_a0=rwkv_a0, rwkv_a1=rwkv_a1, rwkv_a2=rwkv_a2,
             rwkv_v0=rwkv_v0, rwkv_v1=rwkv_v1, rwkv_v2=rwkv_v2, rwkv_k_k=rwkv_k_k,
             rwkv_k_a=rwkv_k_a, rwkv_r_k=rwkv_r_k, rwkv_gn_g=rwkv_gn_g, rwkv_gn_b=rwkv_gn_b,
             rwkv_w_o=rwkv_w_o, pool_norm=pool_norm, pool_w_in=pool_w_in,
             pool_w_grp=pool_w_grp, pool_b_grp=pool_b_grp, pool_scale=pool_scale,
             pool_w_o=pool_w_o, final_norm=final_norm)
    B = x_prompt.shape[0]
    shift0 = jnp.zeros((N_RWKV, B, D_MODEL), state_shift.dtype)
    wkv0 = jnp.zeros((N_RWKV, B, N_HEADS, HEAD_DIM, HEAD_DIM), state_wkv.dtype)
    buf0 = jnp.zeros((N_POOL, B, POOL_BUF, D_INNER), state_pool.dtype)
    y_prompt, sh_p, wkv_p, pool_p = trunk(x_prompt, shift0, wkv0, buf0, 0, p)
    y_sample, sh_s, wkv_s, pool_s = trunk(x_sample, state_shift, state_wkv, state_pool, PAST_LEN, p)
    sh_p = sh_p.astype(state_shift.dtype)
    pool_p = pool_p.astype(state_pool.dtype)
    sh_s = sh_s.astype(state_shift.dtype)
    pool_s = pool_s.astype(state_pool.dtype)
    return (y_prompt, y_sample, sh_p, wkv_p, pool_p, sh_s, wkv_s, pool_s)
```

```python
import functools
import math

import jax
import jax.numpy as jnp
from jax import lax
from jax.experimental import pallas as pl
from jax.experimental.pallas import tpu as pltpu

F32 = jnp.float32
BF16 = jnp.bfloat16

HEAD_DIM = 64
POOL_WINDOWS = (2, 4, 8, 16)
POOL_HALO = 16
NORM_EPS = 1e-6
GN_EPS = 64e-5
PAST_LEN = 16384

VMEM_LIMIT_BYTES = 56 * 1024 * 1024
ROW_TILE = 128
OUT_ROW_TILE = 256
POOL_ROW_TILE = 256
WKV_ROW_TILE = 256
WKV_SHORT_ROW_TILE = 64
WKV_CHUNK = 16
WKV_HEADS_PER_GROUP = 2
WKV_LANES_PER_STEP = 1024


def _bdot(a, b):
    return jnp.dot(a.astype(BF16), b.astype(BF16), preferred_element_type=F32)


def _bdot_nt(a, b):
    return lax.dot_general(a.astype(BF16), b.astype(BF16), (((1,), (1,)), ((), ())),
                           preferred_element_type=F32)


def _bdot_tn(a, b):
    return lax.dot_general(a.astype(BF16), b.astype(BF16), (((0,), (0,)), ((), ())),
                           preferred_element_type=F32)


def _split2(x):
    hi = x.astype(BF16)
    lo = (x - hi.astype(F32)).astype(BF16)
    return hi, lo


def _split3(x):
    hi = x.astype(BF16)
    r1 = x - hi.astype(F32)
    mid = r1.astype(BF16)
    lo = (r1 - mid.astype(F32)).astype(BF16)
    return hi, mid, lo


def _dot_exact_rhs(x, sel):
    hi, lo = _split2(x)
    return (jnp.dot(hi, sel, preferred_element_type=F32)
            + jnp.dot(lo, sel, preferred_element_type=F32))


def _head_sum(x, g_ref, gt_ref):
    s = _dot_exact_rhs(x, g_ref[...])
    return _dot_exact_rhs(s, gt_ref[...])


def _rms_norm(x, g):
    return x * lax.rsqrt(jnp.mean(x * x, axis=-1, keepdims=True) + NORM_EPS) * g


def _sigmoid(x):
    return 1.0 / (1.0 + jnp.exp(-x))


def _const_spec(shape):
    nd = len(shape)
    return pl.BlockSpec(shape, lambda *_: (0,) * nd, pipeline_mode=pl.Buffered(1))


def _params(sem):
    return pltpu.CompilerParams(dimension_semantics=sem, vmem_limit_bytes=VMEM_LIMIT_BYTES)


def _rwkv_proj_kernel(*refs, seq_len, tile, long_mode, has_vres):
    it = iter(refs)
    x_ref = next(it)
    halo_ref = next(it)
    norm_ref, mu_ref, wr_ref, wk_ref, wv_ref, wz_ref = (next(it) for _ in range(6))
    w0_ref, w1_ref, w2_ref, a0_ref, a1_ref, a2_ref = (next(it) for _ in range(6))
    kk_ref, ka_ref, rk_ref, g_ref, gt_ref = (next(it) for _ in range(5))
    if has_vres:
        v0_ref, v1_ref, v2_ref, vfirst_ref = (next(it) for _ in range(4))
    r_o, lw_o, kh_o, v_o, kkn_o, ba_o, bonus_o, sz_o, xn_o = (next(it) for _ in range(9))
    carry_ref = next(it) if long_mode else None

    x = x_ref[...]
    xn = _rms_norm(x, norm_ref[...])
    rolled = pltpu.roll(xn, 1, 0)
    row = lax.broadcasted_iota(jnp.int32, xn.shape, 0)
    if long_mode:
        t = pl.program_id(1)

        @pl.when(t == 0)
        def _():
            carry_ref[...] = jnp.broadcast_to(halo_ref[0], carry_ref.shape)

        x_prev = jnp.where(row == 0, carry_ref[7:8, :], rolled)
        carry_ref[...] = xn[tile - 8:tile, :]
        xn_o[0] = xn[tile - 1:tile, :]
    else:
        x_prev = jnp.where((row & (seq_len - 1)) == 0, halo_ref[...], rolled)
        xn_o[...] = xn
    dx = x_prev - xn
    mu = mu_ref[...]
    xr, xw, xk, xv, xa, xg = (xn + dx * mu[m:m + 1, :] for m in range(6))

    r = _bdot(xr, wr_ref[...])
    k = _bdot(xk, wk_ref[...])
    v = _bdot(xv, wv_ref[...])
    z = _bdot(xg, wz_ref[...])

    wl = w0_ref[...] + _bdot(jnp.tanh(_bdot(xw, w1_ref[...])), w2_ref[...])
    neg = -wl
    softplus = jnp.maximum(neg, 0.0) + jnp.log(1.0 + jnp.exp(-jnp.abs(neg)))
    lw_o[...] = -jnp.exp(-softplus - 0.5)

    a = _sigmoid(a0_ref[...] + _bdot(_bdot(xa, a1_ref[...]), a2_ref[...]))
    if has_vres:
        gate = _sigmoid(v0_ref[...] + _bdot(_bdot(xv, v1_ref[...]), v2_ref[...]))
        v = v + (vfirst_ref[...] - v) * gate

    kk = k * kk_ref[...]
    nrm = jnp.maximum(jnp.sqrt(_head_sum(kk * kk, g_ref, gt_ref)), 1e-12)
    kk = kk / nrm
    kh = k * (1.0 + (a - 1.0) * ka_ref[...])
    bonus_o[...] = _head_sum(r * kh * rk_ref[...], g_ref, gt_ref) * v
    r_o[...] = r
    kh_o[...] = kh
    v_o[...] = v
    kkn_o[...] = kk
    ba_o[...] = kk * a
    sz_o[...] = z * _sigmoid(z)


def _rwkv_proj(x2d, halo, seq_len, p, vres, vfirst):
    rows, d = x2d.shape
    c = p['w_r'].shape[1]
    nb = rows // seq_len
    long_mode = seq_len >= ROW_TILE
    tile = ROW_TILE if long_mode else min(ROW_TILE, rows)
    has_vres = vres is not None
    if long_mode:
        tps = seq_len // tile
        grid = (nb, tps)
        row_map = lambda b, t: (b * tps + t, 0)
        halo_spec = pl.BlockSpec((1, 1, d), lambda b, t: (b, 0, 0))
        xn_shape = jax.ShapeDtypeStruct((nb, 1, d), F32)
        xn_spec = pl.BlockSpec((1, 1, d), lambda b, t: (b, 0, 0))
        scratch = [pltpu.VMEM((8, d), F32)]
        sem = ("arbitrary", "arbitrary")
    else:
        grid = (rows // tile,)
        row_map = lambda i: (i, 0)
        halo_spec = pl.BlockSpec((tile, d), row_map)
        xn_shape = jax.ShapeDtypeStruct((rows, d), F32)
        xn_spec = pl.BlockSpec((tile, d), row_map)
        scratch = []
        sem = ("arbitrary",)
    row_d = pl.BlockSpec((tile, d), row_map)
    row_c = pl.BlockSpec((tile, c), row_map)
    consts = [p['norm'], p['mu'], p['w_r'], p['w_k'], p['w_v'], p['w_z'], p['w0'], p['w1'], p['w2'],
              p['a0'], p['a1'], p['a2'], p['k_k'], p['k_a'], p['r_k'], p['g'], p['gt']]
    args = [x2d, halo] + consts
    in_specs = [row_d, halo_spec] + [_const_spec(a.shape) for a in consts]
    if has_vres:
        args += list(vres) + [vfirst]
        in_specs += [_const_spec(a.shape) for a in vres] + [row_c]
    out_shape = [jax.ShapeDtypeStruct((rows, c), F32)] * 8 + [xn_shape]
    out_specs = [row_c] * 8 + [xn_spec]
    return pl.pallas_call(
        functools.partial(_rwkv_proj_kernel, seq_len=seq_len, tile=tile, long_mode=long_mode,
                          has_vres=has_vres),
        grid=grid, in_specs=in_specs, out_specs=out_specs, out_shape=out_shape,
        scratch_shapes=scratch, compiler_params=_params(sem), name="rwkv_proj",
    )(*args)


def _wkv_chunk(s_bd, r, lw, k, v, kk, ba, consts, chunk, hpg):
    tri3, lane_masks, strict, incl, bmask, eye, state_mask = consts
    hi, mid, lo = _split3(lw)
    c = (jnp.dot(tri3, hi, preferred_element_type=F32)
         + jnp.dot(tri3, mid, preferred_element_type=F32)
         + jnp.dot(tri3, lo, preferred_element_type=F32))
    c_last = c[chunk - 1:chunk, :]
    a_t = -kk * jnp.exp(c - lw)
    r_t = r * jnp.exp(c)
    e_inv = jnp.exp(-c)
    b_t = ba * e_inv
    k_t = k * e_inv
    e_hat = jnp.exp(c_last - c)
    b_h = ba * e_hat
    k_h = k * e_hat

    def stack(m):
        return jnp.concatenate([m * lm for lm in lane_masks], axis=0)

    lhs2 = jnp.concatenate([a_t, r_t], axis=0)
    sc_b = _bdot_nt(lhs2, stack(b_t))
    sc_k = _bdot_nt(lhs2, stack(k_t))
    x_ab = sc_b[:chunk] * strict
    x_rb = sc_b[chunk:] * incl
    x_ak = sc_k[:chunk] * strict
    x_rk = sc_k[chunk:] * incl
    a_bd = jnp.concatenate([x_ab] * hpg, axis=0) * bmask
    t_bd = eye + a_bd
    pw = a_bd
    for _ in range(int(math.log2(chunk)) - 1):
        pw = _bdot(pw, pw)
        t_bd = t_bd + _bdot(t_bd, pw)
    t_row = t_bd[:chunk]
    for e in range(1, hpg):
        t_row = t_row + t_bd[e * chunk:(e + 1) * chunk]
    sr = _bdot_nt(lhs2, s_bd)
    v_st = stack(v)
    xa = sr[:chunk] + _bdot(x_ak, v_st)
    u = _bdot(t_row, stack(xa))
    y = sr[chunk:] + _bdot(x_rb, stack(u)) + _bdot(x_rk, v_st)
    upd = _bdot_tn(jnp.concatenate([u, v], axis=0), jnp.concatenate([b_h, k_h], axis=0))
    s_new = s_bd * jnp.exp(c_last) + upd * state_mask
    return s_new, y


def _wkv_consts(chunk, hpg):
    w = HEAD_DIM * hpg
    n = chunk * hpg
    sh = int(math.log2(chunk))
    ti = lax.broadcasted_iota(jnp.int32, (chunk, chunk), 0)
    si = lax.broadcasted_iota(jnp.int32, (chunk, chunk), 1)
    tri3 = (ti >= si).astype(BF16)
    lane = lax.broadcasted_iota(jnp.int32, (1, w), 1)
    lane_masks = [((lane >> 6) == e).astype(F32) for e in range(hpg)]
    t2 = lax.broadcasted_iota(jnp.int32, (chunk, n), 0)
    s2 = lax.broadcasted_iota(jnp.int32, (chunk, n), 1) & (chunk - 1)
    strict = (s2 < t2).astype(F32)
    incl = (s2 <= t2).astype(F32)
    rb = lax.broadcasted_iota(jnp.int32, (n, n), 0)
    cb = lax.broadcasted_iota(jnp.int32, (n, n), 1)
    bmask = ((rb >> sh) == (cb >> sh)).astype(F32)
    eye = (rb == cb).astype(F32)
    rs = lax.broadcasted_iota(jnp.int32, (w, w), 0)
    cs = lax.broadcasted_iota(jnp.int32, (w, w), 1)
    state_mask = ((rs >> 6) == (cs >> 6)).astype(F32)
    return tri3, lane_masks, strict, incl, bmask, eye, state_mask


def _load_state(s_ref, idx, gi, hpg, lane_masks):
    cat = jnp.concatenate([s_ref[idx, gi * hpg + e] for e in range(hpg)], axis=1)
    return jnp.concatenate([cat * lm for lm in lane_masks], axis=0)


def _store_state(s_ref, idx, gi, hpg, s_bd):
    for e in range(hpg):
        lo = e * HEAD_DIM
        s_ref[idx, gi * hpg + e] = s_bd[lo:lo + HEAD_DIM, lo:lo + HEAD_DIM]


def _wkv_kernel(r_ref, lw_ref, k_ref, v_ref, kk_ref, ba_ref, s_in_ref, y_ref, s_out_ref, *scratch,
                chunk, hpg, groups, tile, long_mode):
    consts = _wkv_consts(chunk, hpg)
    lane_masks = consts[1]
    w = HEAD_DIM * hpg
    n_chunks = tile // chunk

    if long_mode:
        state_ref = scratch[0]
        t = pl.program_id(2)

        @pl.when(t == 0)
        def _():
            for gi in range(groups):
                state_ref[gi] = _load_state(s_in_ref, 0, gi, hpg, lane_masks)

    def body(ci, carry):
        rows = pl.ds(pl.multiple_of(ci * chunk, chunk), chunk)
        for gi in range(groups):
            lanes = slice(gi * w, (gi + 1) * w)
            if long_mode:
                s_bd = state_ref[gi]
            else:
                s_bd = _load_state(s_in_ref, ci, gi, hpg, lane_masks)
            s_new, y = _wkv_chunk(s_bd, r_ref[rows, lanes], lw_ref[rows, lanes], k_ref[rows, lanes],
                                  v_ref[rows, lanes], kk_ref[rows, lanes], ba_ref[rows, lanes],
                                  consts, chunk, hpg)
            y_ref[rows, lanes] = y
            if long_mode:
                state_ref[gi] = s_new
            else:
                _store_state(s_out_ref, ci, gi, hpg, s_new)
        return carry

    lax.fori_loop(0, n_chunks, body, 0)

    if long_mode:
        @pl.when(t == pl.num_programs(2) - 1)
        def _():
            for gi in range(groups):
                _store_state(s_out_ref, 0, gi, hpg, state_ref[gi])


def _wkv(r, lw, kh, v, kk, ba, s_in, seq_len):
    rows, c = r.shape
    nb = rows // seq_len
    hpg = WKV_HEADS_PER_GROUP
    w = HEAD_DIM * hpg
    wb = min(c, WKV_LANES_PER_STEP)
    groups = wb // w
    heads_per_step = wb // HEAD_DIM
    long_mode = seq_len > WKV_CHUNK
    if long_mode:
        chunk = WKV_CHUNK
        tile = min(WKV_ROW_TILE, seq_len)
        tps = seq_len // tile
        grid = (nb, c // wb, tps)
        row_spec = pl.BlockSpec((tile, wb), lambda b, g, t: (b * tps + t, g))
        s_spec = pl.BlockSpec((1, heads_per_step, HEAD_DIM, HEAD_DIM), lambda b, g, t: (b, g, 0, 0))
        scratch = [pltpu.VMEM((groups, w, w), F32)]
        sem = ("arbitrary", "arbitrary", "arbitrary")
    else:
        chunk = seq_len
        tile = min(WKV_SHORT_ROW_TILE, rows)
        seqs = tile // chunk
        grid = (rows // tile, c // wb)
        row_spec = pl.BlockSpec((tile, wb), lambda i, g: (i, g))
        s_spec = pl.BlockSpec((seqs, heads_per_step, HEAD_DIM, HEAD_DIM), lambda i, g: (i, g, 0, 0))
        scratch = []
        sem = ("arbitrary", "arbitrary")
    return pl.pallas_call(
        functools.partial(_wkv_kernel, chunk=chunk, hpg=hpg, groups=groups, tile=tile,
                          long_mode=long_mode),
        grid=grid, in_specs=[row_spec] * 6 + [s_spec], out_specs=[row_spec, s_spec],
        out_shape=[jax.ShapeDtypeStruct((rows, c), F32), jax.ShapeDtypeStruct(s_in.shape, F32)],
        scratch_shapes=scratch, compiler_params=_params(sem), name="wkv7_chunked",
    )(r, lw, kh, v, kk, ba, s_in)


def _rwkv_out_kernel(y_ref, bonus_ref, sz_ref, x_ref, gng_ref, gnb_ref, wo_ref, g_ref, gt_ref, o_ref):
    y = y_ref[...]
    inv_n = 1.0 / HEAD_DIM
    mean = _head_sum(y, g_ref, gt_ref) * inv_n
    d = y - mean
    var = _head_sum(d * d, g_ref, gt_ref) * inv_n
    yn = d * lax.rsqrt(var + GN_EPS) * gng_ref[...] + gnb_ref[...]
    gated = (yn + bonus_ref[...]) * sz_ref[...]
    o_ref[...] = x_ref[...] + _bdot(gated, wo_ref[...])


def _rwkv_out(y, bonus, sz, x2d, p):
    rows, c = y.shape
    d = x2d.shape[1]
    tile = min(OUT_ROW_TILE, rows)
    row_c = pl.BlockSpec((tile, c), lambda i: (i, 0))
    row_d = pl.BlockSpec((tile, d), lambda i: (i, 0))
    consts = [p['gn_g'], p['gn_b'], p['w_o'], p['g'], p['gt']]
    return pl.pallas_call(
        _rwkv_out_kernel, grid=(rows // tile,),
        in_specs=[row_c, row_c, row_c, row_d] + [_const_spec(a.shape) for a in consts],
        out_specs=row_d, out_shape=jax.ShapeDtypeStruct((rows, d), F32),
        compiler_params=_params(("arbitrary",)), name="rwkv_out",
    )(y, bonus, sz, x2d, *consts)


def _pool_kernel(*refs, seq_len, tile, long_mode, pos0, final):
    it = iter(refs)
    x_ref, buf_ref, norm_ref, win_ref, wgrp_ref, bgrp_ref, scale_ref, wo_ref = (next(it) for _ in range(8))
    fnorm_ref = next(it) if final else None
    o_ref, nbuf_ref = next(it), next(it)
    u_ref, ext_ref, p_ref = next(it), next(it), next(it)
    carry_ref = next(it) if long_mode else None

    c = u_ref.shape[1]
    pg = c // len(POOL_WINDOWS)
    tl = tile if long_mode else seq_len
    x = x_ref[...]
    xn = _rms_norm(x, norm_ref[...])
    uz = _bdot(xn, win_ref[...])
    u_ref[...] = uz[:, :c]
    z = uz[:, c:]

    if long_mode:
        t = pl.program_id(1)
        row0 = t * tile
    else:
        row0 = 0
    pos = pos0 + row0 + lax.broadcasted_iota(jnp.int32, (tl, 1), 0)

    def pool_one(base, halo):
        ext_ref[0:POOL_HALO, :] = halo
        ext_ref[POOL_HALO:POOL_HALO + tl, :] = u_ref[pl.ds(base, tl), :]
        for g, win in enumerate(POOL_WINDOWS):
            lanes = slice(g * pg, (g + 1) * pg)
            s = ext_ref[:, lanes]
            dlt = 1
            while dlt < win:
                s = s + pltpu.roll(s, dlt, 0)
                dlt *= 2
            cnt = jnp.minimum(win, pos + 1).astype(F32)
            cur = ext_ref[POOL_HALO:POOL_HALO + tl, lanes]
            p_ref[pl.ds(base, tl), lanes] = s[POOL_HALO:POOL_HALO + tl, :] / cnt - cur
        return ext_ref[tl:tl + POOL_HALO, :]

    if long_mode:
        @pl.when(t == 0)
        def _():
            carry_ref[...] = buf_ref[0]

        tail = pool_one(0, carry_ref[...])
        carry_ref[...] = tail
        nbuf_ref[0] = tail
    else:
        def body(s, carry):
            base = pl.multiple_of(s * seq_len, seq_len)
            nbuf_ref[s] = pool_one(base, buf_ref[s])
            return carry

        lax.fori_loop(0, tile // seq_len, body, 0)

    p = p_ref[...]
    mixed = jnp.concatenate(
        [_bdot(p[:, g * pg:(g + 1) * pg], wgrp_ref[g]) for g in range(len(POOL_WINDOWS))], axis=1)
    mixed = (mixed + bgrp_ref[...]) * scale_ref[...]
    gated = mixed * (z * _sigmoid(z))
    out = x + _bdot(gated, wo_ref[...])
    if final:
        out = _rms_norm(out, fnorm_ref[...])
    o_ref[...] = out


def _pool_layer(x2d, buf16, seq_len, pos0, p, final_norm):
    rows, d = x2d.shape
    c = p['w_o'].shape[0]
    nb = rows // seq_len
    long_mode = seq_len >= POOL_ROW_TILE
    final = final_norm is not None
    if long_mode:
        tile = POOL_ROW_TILE
        tps = seq_len // tile
        grid = (nb, tps)
        row_spec = pl.BlockSpec((tile, d), lambda b, t: (b * tps + t, 0))
        buf_spec = pl.BlockSpec((1, POOL_HALO, c), lambda b, t: (b, 0, 0))
        tl = tile
        sem = ("arbitrary", "arbitrary")
    else:
        tile = min(POOL_ROW_TILE, rows)
        grid = (rows // tile,)
        row_spec = pl.BlockSpec((tile, d), lambda i: (i, 0))
        buf_spec = pl.BlockSpec((tile // seq_len, POOL_HALO, c), lambda i: (i, 0, 0))
        tl = seq_len
        sem = ("arbitrary",)
    consts = [p['norm'], p['w_in'], p['w_grp'], p['b_grp'], p['scale'], p['w_o']]
    if final:
        consts.append(final_norm)
    scratch = [pltpu.VMEM((tile, c), F32), pltpu.VMEM((POOL_HALO + tl, c), F32), pltpu.VMEM((tile, c), F32)]
    if long_mode:
        scratch.append(pltpu.VMEM((POOL_HALO, c), F32))
    return pl.pallas_call(
        functools.partial(_pool_kernel, seq_len=seq_len, tile=tile, long_mode=long_mode, pos0=pos0,
                          final=final),
        grid=grid, in_specs=[row_spec, buf_spec] + [_const_spec(a.shape) for a in consts],
        out_specs=[row_spec, buf_spec],
        out_shape=[jax.ShapeDtypeStruct((rows, d), F32), jax.ShapeDtypeStruct((nb, POOL_HALO, c), F32)],
        scratch_shapes=scratch, compiler_params=_params(sem), name="pool_mixer",
    )(x2d, buf16, *consts)


def _trunk(x, shift0, wkv0, buf0, pos0, rw, pw, vres, final_norm):
    nb, seq_len, d = x.shape
    x2d = x.reshape(nb * seq_len, d)
    new_shift, new_wkv, new_buf = [], [], []
    v_first = None
    depth = len(rw) + len(pw)
    for i in range(depth):
        j = i // 2
        if i % 2 == 0:
            p = rw[j]
            if seq_len >= ROW_TILE:
                halo = shift0[j][:, None, :]
            else:
                halo = jnp.repeat(shift0[j], seq_len, axis=0)
            r, lw, kh, v, kk, ba, bonus, sz, xn_last = _rwkv_proj(
                x2d, halo, seq_len, p, None if j == 0 else vres[j - 1], v_first)
            if j == 0:
                v_first = v
            y, s_new = _wkv(r, lw, kh, v, kk, ba, wkv0[j], seq_len)
            x2d = _rwkv_out(y, bonus, sz, x2d, p)
            new_shift.append(xn_last.reshape(nb, -1, d)[:, -1, :])
            new_wkv.append(s_new)
        else:
            p = pw[j]
            buf16 = jnp.pad(buf0[j], ((0, 0), (1, 0), (0, 0)))
            x2d, nbuf = _pool_layer(x2d, buf16, seq_len, pos0, p,
                                    final_norm if i == depth - 1 else None)
            new_buf.append(nbuf[:, 1:, :])
    return (x2d.reshape(nb, seq_len, d), jnp.stack(new_shift), jnp.stack(new_wkv), jnp.stack(new_buf))


def kernel(x_prompt, x_sample, state_shift, state_wkv, state_pool, rwkv_norm, rwkv_mu, rwkv_w_r, rwkv_w_k, rwkv_w_v, rwkv_w_z, rwkv_w0, rwkv_w1, rwkv_w2, rwkv_a0, rwkv_a1, rwkv_a2, rwkv_v0, rwkv_v1, rwkv_v2, rwkv_k_k, rwkv_k_a, rwkv_r_k, rwkv_gn_g, rwkv_gn_b, rwkv_w_o, pool_norm, pool_w_in, pool_w_grp, pool_b_grp, pool_scale, pool_w_o, final_norm):
    n_rwkv, d, c = rwkv_w_r.shape
    n_pool = pool_w_in.shape[0]
    n_heads = c // HEAD_DIM
    bf = lambda a: a.astype(BF16)
    row = lambda a: a.reshape(1, -1)
    head_of_lane = jnp.arange(c, dtype=jnp.int32) // HEAD_DIM
    g = (head_of_lane[:, None] == jnp.arange(n_heads, dtype=jnp.int32)[None, :]).astype(BF16)
    gt = g.T
    rw = [dict(norm=row(rwkv_norm[j]), mu=rwkv_mu[j], w_r=bf(rwkv_w_r[j]), w_k=bf(rwkv_w_k[j]),
               w_v=bf(rwkv_w_v[j]), w_z=bf(rwkv_w_z[j]), w0=row(rwkv_w0[j]), w1=bf(rwkv_w1[j]),
               w2=bf(rwkv_w2[j]), a0=row(rwkv_a0[j]), a1=bf(rwkv_a1[j]), a2=bf(rwkv_a2[j]),
               k_k=row(rwkv_k_k[j]), k_a=row(rwkv_k_a[j]), r_k=row(rwkv_r_k[j]),
               gn_g=row(rwkv_gn_g[j]), gn_b=row(rwkv_gn_b[j]), w_o=bf(rwkv_w_o[j]), g=g, gt=gt)
          for j in range(n_rwkv)]
    vres = [(row(rwkv_v0[j]), bf(rwkv_v1[j]), bf(rwkv_v2[j])) for j in range(rwkv_v0.shape[0])]
    pw = [dict(norm=row(pool_norm[j]), w_in=bf(pool_w_in[j]), w_grp=bf(pool_w_grp[j]),
               b_grp=row(pool_b_grp[j]), scale=row(pool_scale[j]), w_o=bf(pool_w_o[j]))
          for j in range(n_pool)]
    fn = row(final_norm)

    nb = x_prompt.shape[0]
    shift0 = jnp.zeros((n_rwkv, nb, d), state_shift.dtype)
    wkv0 = jnp.zeros((n_rwkv, nb, n_heads, HEAD_DIM, HEAD_DIM), state_wkv.dtype)
    buf0 = jnp.zeros((n_pool, nb, state_pool.shape[2], c), state_pool.dtype)
    y_p, sh_p, wkv_p, pool_p = _trunk(x_prompt, shift0, wkv0, buf0, 0, rw, pw, vres, fn)
    y_s, sh_s, wkv_s, pool_s = _trunk(x_sample, state_shift, state_wkv, state_pool, PAST_LEN, rw, pw,
                                      vres, fn)
    return (y_p, y_s, sh_p, wkv_p, pool_p, sh_s, wkv_s, pool_s)
```

```python
import functools
import math

import jax
import jax.numpy as jnp
from jax import lax
from jax.experimental import pallas as pl
from jax.experimental.pallas import tpu as pltpu

F32 = jnp.float32
BF16 = jnp.bfloat16

HEAD_DIM = 64
POOL_WINDOWS = (2, 4, 8, 16)
POOL_HALO = 16
NORM_EPS = 1e-6
GN_EPS = 64e-5
PAST_LEN = 16384

VMEM_LIMIT_BYTES = 56 * 1024 * 1024
ROW_TILE = 128
OUT_ROW_TILE = 256
POOL_ROW_TILE = 256
WKV_ROW_TILE = 64
WKV_SEQS_PER_STEP = 2
WKV_SHORT_SEQS_PER_STEP = 4
WKV_CHUNK = 16
WKV_HEADS_PER_GROUP = 4


def _bdot(a, b):
    return jnp.dot(a.astype(BF16), b.astype(BF16), preferred_element_type=F32)


def _bdot_nt(a, b):
    return lax.dot_general(a.astype(BF16), b.astype(BF16), (((1,), (1,)), ((), ())),
                           preferred_element_type=F32)


def _bdot_tn(a, b):
    return lax.dot_general(a.astype(BF16), b.astype(BF16), (((0,), (0,)), ((), ())),
                           preferred_element_type=F32)


def _split2(x):
    hi = x.astype(BF16)
    lo = (x - hi.astype(F32)).astype(BF16)
    return hi, lo


def _split3(x):
    hi = x.astype(BF16)
    r1 = x - hi.astype(F32)
    mid = r1.astype(BF16)
    lo = (r1 - mid.astype(F32)).astype(BF16)
    return hi, mid, lo


def _dot_exact_rhs(x, sel):
    hi, lo = _split2(x)
    return (jnp.dot(hi, sel, preferred_element_type=F32)
            + jnp.dot(lo, sel, preferred_element_type=F32))


def _head_sum(x, g_ref, gt_ref):
    s = _dot_exact_rhs(x, g_ref[...])
    return _dot_exact_rhs(s, gt_ref[...])


def _sum(xs):
    return functools.reduce(lambda a, b: a + b, xs)


def _rms_norm(x, g):
    return x * lax.rsqrt(jnp.mean(x * x, axis=-1, keepdims=True) + NORM_EPS) * g


def _sigmoid(x):
    return 1.0 / (1.0 + jnp.exp(-x))


def _const_spec(shape):
    nd = len(shape)
    return pl.BlockSpec(shape, lambda *_: (0,) * nd, pipeline_mode=pl.Buffered(1))


def _params(sem):
    return pltpu.CompilerParams(dimension_semantics=sem, vmem_limit_bytes=VMEM_LIMIT_BYTES)


def _rwkv_proj_kernel(*refs, seq_len, tile, long_mode, has_vres):
    it = iter(refs)
    x_ref = next(it)
    halo_ref = next(it)
    norm_ref, mu_ref, wr_ref, wk_ref, wv_ref, wz_ref = (next(it) for _ in range(6))
    w0_ref, w1_ref, w2_ref, a0_ref, a1_ref, a2_ref = (next(it) for _ in range(6))
    kk_ref, ka_ref, rk_ref, g_ref, gt_ref = (next(it) for _ in range(5))
    if has_vres:
        v0_ref, v1_ref, v2_ref, vfirst_ref = (next(it) for _ in range(4))
    r_o, lw_o, kh_o, v_o, kkn_o, ba_o, bonus_o, sz_o, xn_o = (next(it) for _ in range(9))
    carry_ref = next(it) if long_mode else None

    x = x_ref[...]
    xn = _rms_norm(x, norm_ref[...])
    rolled = pltpu.roll(xn, 1, 0)
    row = lax.broadcasted_iota(jnp.int32, xn.shape, 0)
    if long_mode:
        t = pl.program_id(1)

        @pl.when(t == 0)
        def _():
            carry_ref[...] = jnp.broadcast_to(halo_ref[0], carry_ref.shape)

        x_prev = jnp.where(row == 0, carry_ref[7:8, :], rolled)
        carry_ref[...] = xn[tile - 8:tile, :]
        xn_o[0] = xn[tile - 1:tile, :]
    else:
        x_prev = jnp.where((row & (seq_len - 1)) == 0, halo_ref[...], rolled)
        xn_o[...] = xn
    dx = x_prev - xn
    mu = mu_ref[...]
    xr, xw, xk, xv, xa, xg = (xn + dx * mu[m:m + 1, :] for m in range(6))

    r = _bdot(xr, wr_ref[...])
    k = _bdot(xk, wk_ref[...])
    v = _bdot(xv, wv_ref[...])
    z = _bdot(xg, wz_ref[...])

    wl = w0_ref[...] + _bdot(jnp.tanh(_bdot(xw, w1_ref[...])), w2_ref[...])
    neg = -wl
    softplus = jnp.maximum(neg, 0.0) + jnp.log(1.0 + jnp.exp(-jnp.abs(neg)))
    lw_o[...] = -jnp.exp(-softplus - 0.5)

    a = _sigmoid(a0_ref[...] + _bdot(_bdot(xa, a1_ref[...]), a2_ref[...]))
    if has_vres:
        gate = _sigmoid(v0_ref[...] + _bdot(_bdot(xv, v1_ref[...]), v2_ref[...]))
        v = v + (vfirst_ref[...] - v) * gate

    kk = k * kk_ref[...]
    nrm = jnp.maximum(jnp.sqrt(_head_sum(kk * kk, g_ref, gt_ref)), 1e-12)
    kk = kk / nrm
    kh = k * (1.0 + (a - 1.0) * ka_ref[...])
    bonus_o[...] = _head_sum(r * kh * rk_ref[...], g_ref, gt_ref) * v
    r_o[...] = r
    kh_o[...] = kh
    v_o[...] = v
    kkn_o[...] = kk
    ba_o[...] = kk * a
    sz_o[...] = z * _sigmoid(z)


def _rwkv_proj(x2d, halo, seq_len, p, vres, vfirst):
    rows, d = x2d.shape
    c = p['w_r'].shape[1]
    nb = rows // seq_len
    long_mode = seq_len >= ROW_TILE
    tile = ROW_TILE if long_mode else min(ROW_TILE, rows)
    has_vres = vres is not None
    if long_mode:
        tps = seq_len // tile
        grid = (nb, tps)
        row_map = lambda b, t: (b * tps + t, 0)
        halo_spec = pl.BlockSpec((1, 1, d), lambda b, t: (b, 0, 0))
        xn_shape = jax.ShapeDtypeStruct((nb, 1, d), F32)
        xn_spec = pl.BlockSpec((1, 1, d), lambda b, t: (b, 0, 0))
        scratch = [pltpu.VMEM((8, d), F32)]
        sem = ("arbitrary", "arbitrary")
    else:
        grid = (rows // tile,)
        row_map = lambda i: (i, 0)
        halo_spec = pl.BlockSpec((tile, d), row_map)
        xn_shape = jax.ShapeDtypeStruct((rows, d), F32)
        xn_spec = pl.BlockSpec((tile, d), row_map)
        scratch = []
        sem = ("arbitrary",)
    row_d = pl.BlockSpec((tile, d), row_map)
    row_c = pl.BlockSpec((tile, c), row_map)
    consts = [p['norm'], p['mu'], p['w_r'], p['w_k'], p['w_v'], p['w_z'], p['w0'], p['w1'], p['w2'],
              p['a0'], p['a1'], p['a2'], p['k_k'], p['k_a'], p['r_k'], p['g'], p['gt']]
    args = [x2d, halo] + consts
    in_specs = [row_d, halo_spec] + [_const_spec(a.shape) for a in consts]
    if has_vres:
        args += list(vres) + [vfirst]
        in_specs += [_const_spec(a.shape) for a in vres] + [row_c]
    out_shape = [jax.ShapeDtypeStruct((rows, c), F32)] * 8 + [xn_shape]
    out_specs = [row_c] * 8 + [xn_spec]
    return pl.pallas_call(
        functools.partial(_rwkv_proj_kernel, seq_len=seq_len, tile=tile, long_mode=long_mode,
                          has_vres=has_vres),
        grid=grid, in_specs=in_specs, out_specs=out_specs, out_shape=out_shape,
        scratch_shapes=scratch, compiler_params=_params(sem), name="rwkv_proj",
    )(*args)


def _wkv_masks(chunk, hpg, op_dtype):
    w = HEAD_DIM * hpg
    n = chunk * hpg
    sh = int(math.log2(chunk))
    lane = lax.broadcasted_iota(jnp.int32, (1, w), 1)
    lane_masks = [((lane >> 6) == e).astype(F32) for e in range(hpg)]
    op_masks = [m.astype(op_dtype) for m in lane_masks]
    t2 = lax.broadcasted_iota(jnp.int32, (chunk, n), 0)
    s2 = lax.broadcasted_iota(jnp.int32, (chunk, n), 1) & (chunk - 1)
    strict = (s2 < t2).astype(F32)
    incl = (s2 <= t2).astype(F32)
    rb = lax.broadcasted_iota(jnp.int32, (n, n), 0)
    cb = lax.broadcasted_iota(jnp.int32, (n, n), 1)
    bmask = ((rb >> sh) == (cb >> sh)).astype(F32)
    eye = (rb == cb).astype(F32)
    rs = lax.broadcasted_iota(jnp.int32, (w, w), 0)
    cs = lax.broadcasted_iota(jnp.int32, (w, w), 1)
    state_mask = ((rs >> 6) == (cs >> 6)).astype(F32)
    return lane_masks, op_masks, strict, incl, bmask, eye, state_mask


def _load_state(s_ref, seq, gi, hpg, lane_masks):
    cat = jnp.concatenate([s_ref[seq, gi * hpg + e] for e in range(hpg)], axis=1)
    return jnp.concatenate([cat * lm for lm in lane_masks], axis=0)


def _store_state(s_ref, seq, gi, hpg, s_bd):
    for e in range(hpg):
        lo = e * HEAD_DIM
        s_ref[seq, gi * hpg + e] = s_bd[lo:lo + HEAD_DIM, lo:lo + HEAD_DIM]


def _wkv_prepass(r_ref, lw_ref, k_ref, v_ref, kk_ref, ba_ref, ops, clast_ref, chunk):
    at_ref, rt_ref, bt_ref, kt_ref, bh_ref, kh_ref, vb_ref = ops
    nr, c_dim = at_ref.shape
    flat = lambda ref: ref[...].reshape(nr, c_dim)
    sh = int(math.log2(chunk))
    ti = lax.broadcasted_iota(jnp.int32, (nr, nr), 0)
    si = lax.broadcasted_iota(jnp.int32, (nr, nr), 1)
    same = (ti >> sh) == (si >> sh)
    tri = (same & (ti >= si)).astype(BF16)
    ones = same.astype(BF16)
    lw = flat(lw_ref)
    parts = _split3(lw)
    c = _sum([jnp.dot(tri, q, preferred_element_type=F32) for q in parts])
    c_last = _sum([jnp.dot(ones, q, preferred_element_type=F32) for q in parts])
    dt = at_ref.dtype
    kk = flat(kk_ref)
    ba = flat(ba_ref)
    k = flat(k_ref)
    at_ref[...] = (-kk * jnp.exp(c - lw)).astype(dt)
    rt_ref[...] = (flat(r_ref) * jnp.exp(c)).astype(dt)
    e_inv = jnp.exp(-c)
    bt_ref[...] = (ba * e_inv).astype(dt)
    kt_ref[...] = (k * e_inv).astype(dt)
    e_hat = jnp.exp(c_last - c)
    bh_ref[...] = (ba * e_hat).astype(dt)
    kh_ref[...] = (k * e_hat).astype(dt)
    vb_ref[...] = flat(v_ref).astype(dt)
    clast_ref[...] = c_last


def _wkv_chunk_step(states, row_starts, ops, clast_ref, masks, chunk, hpg, groups):
    at_ref, rt_ref, bt_ref, kt_ref, bh_ref, kh_ref, vb_ref = ops
    _, op_masks, strict, incl, bmask, eye, state_mask = masks
    w = HEAD_DIM * hpg
    chains = [(s, g) for s in range(len(row_starts)) for g in range(groups)]

    def tile(ref, s, g):
        return ref[pl.ds(row_starts[s], chunk), g * w:(g + 1) * w]

    def stack(m):
        return jnp.concatenate([m.astype(lm.dtype) * lm for lm in op_masks], axis=0)

    lhs2 = [jnp.concatenate([tile(at_ref, s, g), tile(rt_ref, s, g)], axis=0) for s, g in chains]
    v_t = [tile(vb_ref, s, g) for s, g in chains]
    v_st = [stack(x) for x in v_t]
    sc_b = [_bdot_nt(l, stack(tile(bt_ref, s, g))) for l, (s, g) in zip(lhs2, chains)]
    sc_k = [_bdot_nt(l, stack(tile(kt_ref, s, g))) for l, (s, g) in zip(lhs2, chains)]
    sr = [_bdot_nt(l, states[s][g]) for l, (s, g) in zip(lhs2, chains)]

    x_rb = [m[chunk:] * incl for m in sc_b]
    x_ak = [m[:chunk] * strict for m in sc_k]
    x_rk = [m[chunk:] * incl for m in sc_k]
    a_bd = [jnp.concatenate([m[:chunk] * strict] * hpg, axis=0) * bmask for m in sc_b]

    pw = [_bdot(a, a) for a in a_bd]
    xa = [m[:chunk] + _bdot(x, vs) for m, x, vs in zip(sr, x_ak, v_st)]
    y_k = [m[chunk:] + _bdot(x, vs) for m, x, vs in zip(sr, x_rk, v_st)]
    t_bd = [eye + a for a in a_bd]
    for _ in range(int(math.log2(chunk)) - 2):
        t_new = [t + _bdot(p, t) for t, p in zip(t_bd, pw)]
        pw = [_bdot(p, p) for p in pw]
        t_bd = t_new
    t_bd = [t + _bdot(p, t) for t, p in zip(t_bd, pw)]
    t_row = [_sum([t[e * chunk:(e + 1) * chunk] for e in range(hpg)]) for t in t_bd]

    u = [_bdot(t, stack(x)) for t, x in zip(t_row, xa)]
    y = [yk + _bdot(x, stack(uu)) for yk, x, uu in zip(y_k, x_rb, u)]
    upd = [_bdot_tn(jnp.concatenate([uu.astype(vv.dtype), vv], axis=0),
                    jnp.concatenate([tile(bh_ref, s, g), tile(kh_ref, s, g)], axis=0))
           for uu, vv, (s, g) in zip(u, v_t, chains)]

    new_states = [[None] * groups for _ in row_starts]
    ys = [[None] * groups for _ in row_starts]
    for (s, g), up, yy in zip(chains, upd, y):
        decay = jnp.exp(clast_ref[pl.ds(row_starts[s], 1), g * w:(g + 1) * w])
        new_states[s][g] = states[s][g] * decay + up * state_mask
        ys[s][g] = yy
    return new_states, ys


def _wkv_kernel(r_ref, lw_ref, k_ref, v_ref, kk_ref, ba_ref, s_in_ref, y_ref, s_out_ref, *scratch,
                chunk, hpg, groups, nseq, rows, long_mode):
    ops = scratch[:7]
    clast_ref = scratch[7]
    w = HEAD_DIM * hpg
    _wkv_prepass(r_ref, lw_ref, k_ref, v_ref, kk_ref, ba_ref, ops, clast_ref, chunk)
    masks = _wkv_masks(chunk, hpg, ops[0].dtype)
    lane_masks = masks[0]

    if long_mode:
        state_ref = scratch[8]
        t = pl.program_id(1)

        @pl.when(t == 0)
        def _():
            for s in range(nseq):
                for g in range(groups):
                    state_ref[s, g] = _load_state(s_in_ref, s, g, hpg, lane_masks)

        def body(ci, carry):
            starts = [pl.multiple_of(s * rows + ci * chunk, chunk) for s in range(nseq)]
            states = [[state_ref[s, g] for g in range(groups)] for s in range(nseq)]
            new_states, ys = _wkv_chunk_step(states, starts, ops, clast_ref, masks, chunk, hpg, groups)
            for s in range(nseq):
                for g in range(groups):
                    state_ref[s, g] = new_states[s][g]
                    y_ref[s, pl.ds(pl.multiple_of(ci * chunk, chunk), chunk), g * w:(g + 1) * w] = ys[s][g]
            return carry

        lax.fori_loop(0, rows // chunk, body, 0)

        @pl.when(t == pl.num_programs(1) - 1)
        def _():
            for s in range(nseq):
                for g in range(groups):
                    _store_state(s_out_ref, s, g, hpg, state_ref[s, g])
    else:
        starts = [s * rows for s in range(nseq)]
        states = [[_load_state(s_in_ref, s, g, hpg, lane_masks) for g in range(groups)]
                  for s in range(nseq)]
        new_states, ys = _wkv_chunk_step(states, starts, ops, clast_ref, masks, chunk, hpg, groups)
        for s in range(nseq):
            for g in range(groups):
                _store_state(s_out_ref, s, g, hpg, new_states[s][g])
                y_ref[s, :, g * w:(g + 1) * w] = ys[s][g]


def _wkv(r, lw, kh, v, kk, ba, s_in, seq_len):
    total, c = r.shape
    nb = total // seq_len
    hpg = WKV_HEADS_PER_GROUP
    groups = c // (HEAD_DIM * hpg)
    n_heads = c // HEAD_DIM
    long_mode = seq_len > WKV_CHUNK
    if long_mode:
        chunk, nseq, rows = WKV_CHUNK, WKV_SEQS_PER_STEP, min(WKV_ROW_TILE, seq_len)
        tps = seq_len // rows
        grid = (nb // nseq, tps)
        row_spec = pl.BlockSpec((nseq, None, rows, c), lambda b, t: (b, t, 0, 0))
        s_spec = pl.BlockSpec((nseq, n_heads, HEAD_DIM, HEAD_DIM), lambda b, t: (b, 0, 0, 0))
        shape4 = (nb, tps, rows, c)
        op_dtype = BF16
        sem = ("arbitrary", "arbitrary")
    else:
        chunk, nseq, rows = seq_len, WKV_SHORT_SEQS_PER_STEP, seq_len
        grid = (nb // nseq,)
        row_spec = pl.BlockSpec((nseq, None, rows, c), lambda b: (b, 0, 0, 0))
        s_spec = pl.BlockSpec((nseq, n_heads, HEAD_DIM, HEAD_DIM), lambda b: (b, 0, 0, 0))
        shape4 = (nb, 1, rows, c)
        op_dtype = F32
        sem = ("arbitrary",)
    scratch = [pltpu.VMEM((nseq * rows, c), op_dtype)] * 7 + [pltpu.VMEM((nseq * rows, c), F32)]
    if long_mode:
        scratch.append(pltpu.VMEM((nseq, groups, HEAD_DIM * hpg, HEAD_DIM * hpg), F32))
    ins = [a.reshape(shape4) for a in (r, lw, kh, v, kk, ba)]
    y, s_out = pl.pallas_call(
        functools.partial(_wkv_kernel, chunk=chunk, hpg=hpg, groups=groups, nseq=nseq, rows=rows,
                          long_mode=long_mode),
        grid=grid, in_specs=[row_spec] * 6 + [s_spec], out_specs=[row_spec, s_spec],
        out_shape=[jax.ShapeDtypeStruct(shape4, F32), jax.ShapeDtypeStruct(s_in.shape, F32)],
        scratch_shapes=scratch, compiler_params=_params(sem), name="wkv7_chunked",
    )(*ins, s_in)
    return y.reshape(total, c), s_out


def _rwkv_out_kernel(y_ref, bonus_ref, sz_ref, x_ref, gng_ref, gnb_ref, wo_ref, g_ref, gt_ref, o_ref):
    y = y_ref[...]
    inv_n = 1.0 / HEAD_DIM
    mean = _head_sum(y, g_ref, gt_ref) * inv_n
    d = y - mean
    var = _head_sum(d * d, g_ref, gt_ref) * inv_n
    yn = d * lax.rsqrt(var + GN_EPS) * gng_ref[...] + gnb_ref[...]
    gated = (yn + bonus_ref[...]) * sz_ref[...]
    o_ref[...] = x_ref[...] + _bdot(gated, wo_ref[...])


def _rwkv_out(y, bonus, sz, x2d, p):
    rows, c = y.shape
    d = x2d.shape[1]
    tile = min(OUT_ROW_TILE, rows)
    row_c = pl.BlockSpec((tile, c), lambda i: (i, 0))
    row_d = pl.BlockSpec((tile, d), lambda i: (i, 0))
    consts = [p['gn_g'], p['gn_b'], p['w_o'], p['g'], p['gt']]
    return pl.pallas_call(
        _rwkv_out_kernel, grid=(rows // tile,),
        in_specs=[row_c, row_c, row_c, row_d] + [_const_spec(a.shape) for a in consts],
        out_specs=row_d, out_shape=jax.ShapeDtypeStruct((rows, d), F32),
        compiler_params=_params(("arbitrary",)), name="rwkv_out",
    )(y, bonus, sz, x2d, *consts)


def _pool_kernel(*refs, seq_len, tile, long_mode, pos0, final):
    it = iter(refs)
    x_ref, buf_ref, norm_ref, win_ref, wgrp_ref, bgrp_ref, scale_ref, wo_ref = (next(it) for _ in range(8))
    fnorm_ref = next(it) if final else None
    o_ref, nbuf_ref = next(it), next(it)
    u_ref, ext_ref, p_ref = next(it), next(it), next(it)
    carry_ref = next(it) if long_mode else None

    c = u_ref.shape[1]
    pg = c // len(POOL_WINDOWS)
    tl = tile if long_mode else seq_len
    x = x_ref[...]
    xn = _rms_norm(x, norm_ref[...])
    uz = _bdot(xn, win_ref[...])
    u_ref[...] = uz[:, :c]
    z = uz[:, c:]

    if long_mode:
        t = pl.program_id(1)
        row0 = t * tile
    else:
        row0 = 0
    pos = pos0 + row0 + lax.broadcasted_iota(jnp.int32, (tl, 1), 0)

    def pool_one(base, halo):
        ext_ref[0:POOL_HALO, :] = halo
        ext_ref[POOL_HALO:POOL_HALO + tl, :] = u_ref[pl.ds(base, tl), :]
        for g, win in enumerate(POOL_WINDOWS):
            lanes = slice(g * pg, (g + 1) * pg)
            s = ext_ref[:, lanes]
            dlt = 1
            while dlt < win:
                s = s + pltpu.roll(s, dlt, 0)
                dlt *= 2
            cnt = jnp.minimum(win, pos + 1).astype(F32)
            cur = ext_ref[POOL_HALO:POOL_HALO + tl, lanes]
            p_ref[pl.ds(base, tl), lanes] = s[POOL_HALO:POOL_HALO + tl, :] / cnt - cur
        return ext_ref[tl:tl + POOL_HALO, :]

    if long_mode:
        @pl.when(t == 0)
        def _():
            carry_ref[...] = buf_ref[0]

        tail = pool_one(0, carry_ref[...])
        carry_ref[...] = tail
        nbuf_ref[0] = tail
    else:
        def body(s, carry):
            base = pl.multiple_of(s * seq_len, seq_len)
            nbuf_ref[s] = pool_one(base, buf_ref[s])
            return carry

        lax.fori_loop(0, tile // seq_len, body, 0)

    p = p_ref[...]
    mixed = jnp.concatenate(
        [_bdot(p[:, g * pg:(g + 1) * pg], wgrp_ref[g]) for g in range(len(POOL_WINDOWS))], axis=1)
    mixed = (mixed + bgrp_ref[...]) * scale_ref[...]
    gated = mixed * (z * _sigmoid(z))
    out = x + _bdot(gated, wo_ref[...])
    if final:
        out = _rms_norm(out, fnorm_ref[...])
    o_ref[...] = out


def _pool_layer(x2d, buf16, seq_len, pos0, p, final_norm):
    rows, d = x2d.shape
    c = p['w_o'].shape[0]
    nb = rows // seq_len
    long_mode = seq_len >= POOL_ROW_TILE
    final = final_norm is not None
    if long_mode:
        tile = POOL_ROW_TILE
        tps = seq_len // tile
        grid = (nb, tps)
        row_spec = pl.BlockSpec((tile, d), lambda b, t: (b * tps + t, 0))
        buf_spec = pl.BlockSpec((1, POOL_HALO, c), lambda b, t: (b, 0, 0))
        tl = tile
        sem = ("arbitrary", "arbitrary")
    else:
        tile = min(POOL_ROW_TILE, rows)
        grid = (rows // tile,)
        row_spec = pl.BlockSpec((tile, d), lambda i: (i, 0))
        buf_spec = pl.BlockSpec((tile // seq_len, POOL_HALO, c), lambda i: (i, 0, 0))
        tl = seq_len
        sem = ("arbitrary",)
    consts = [p['norm'], p['w_in'], p['w_grp'], p['b_grp'], p['scale'], p['w_o']]
    if final:
        consts.append(final_norm)
    scratch = [pltpu.VMEM((tile, c), F32), pltpu.VMEM((POOL_HALO + tl, c), F32), pltpu.VMEM((tile, c), F32)]
    if long_mode:
        scratch.append(pltpu.VMEM((POOL_HALO, c), F32))
    return pl.pallas_call(
        functools.partial(_pool_kernel, seq_len=seq_len, tile=tile, long_mode=long_mode, pos0=pos0,
                          final=final),
        grid=grid, in_specs=[row_spec, buf_spec] + [_const_spec(a.shape) for a in consts],
        out_specs=[row_spec, buf_spec],
        out_shape=[jax.ShapeDtypeStruct((rows, d), F32), jax.ShapeDtypeStruct((nb, POOL_HALO, c), F32)],
        scratch_shapes=scratch, compiler_params=_params(sem), name="pool_mixer",
    )(x2d, buf16, *consts)


def _trunk(x, shift0, wkv0, buf0, pos0, rw, pw, vres, final_norm):
    nb, seq_len, d = x.shape
    x2d = x.reshape(nb * seq_len, d)
    new_shift, new_wkv, new_buf = [], [], []
    v_first = None
    depth = len(rw) + len(pw)
    for i in range(depth):
        j = i // 2
        if i % 2 == 0:
            p = rw[j]
            if seq_len >= ROW_TILE:
                halo = shift0[j][:, None, :]
            else:
                halo = jnp.repeat(shift0[j], seq_len, axis=0)
            r, lw, kh, v, kk, ba, bonus, sz, xn_last = _rwkv_proj(
                x2d, halo, seq_len, p, None if j == 0 else vres[j - 1], v_first)
            if j == 0:
                v_first = v
            y, s_new = _wkv(r, lw, kh, v, kk, ba, wkv0[j], seq_len)
            x2d = _rwkv_out(y, bonus, sz, x2d, p)
            new_shift.append(xn_last.reshape(nb, -1, d)[:, -1, :])
            new_wkv.append(s_new)
        else:
            p = pw[j]
            buf16 = jnp.pad(buf0[j], ((0, 0), (1, 0), (0, 0)))
            x2d, nbuf = _pool_layer(x2d, buf16, seq_len, pos0, p,
                                    final_norm if i == depth - 1 else None)
            new_buf.append(nbuf[:, 1:, :])
    return (x2d.reshape(nb, seq_len, d), jnp.stack(new_shift), jnp.stack(new_wkv), jnp.stack(new_buf))


def kernel(x_prompt, x_sample, state_shift, state_wkv, state_pool, rwkv_norm, rwkv_mu, rwkv_w_r, rwkv_w_k, rwkv_w_v, rwkv_w_z, rwkv_w0, rwkv_w1, rwkv_w2, rwkv_a0, rwkv_a1, rwkv_a2, rwkv_v0, rwkv_v1, rwkv_v2, rwkv_k_k, rwkv_k_a, rwkv_r_k, rwkv_gn_g, rwkv_gn_b, rwkv_w_o, pool_norm, pool_w_in, pool_w_grp, pool_b_grp, pool_scale, pool_w_o, final_norm):
    n_rwkv, d, c = rwkv_w_r.shape
    n_pool = pool_w_in.shape[0]
    n_heads = c // HEAD_DIM
    bf = lambda a: a.astype(BF16)
    row = lambda a: a.reshape(1, -1)
    head_of_lane = jnp.arange(c, dtype=jnp.int32) // HEAD_DIM
    g = (head_of_lane[:, None] == jnp.arange(n_heads, dtype=jnp.int32)[None, :]).astype(BF16)
    gt = g.T
    rw = [dict(norm=row(rwkv_norm[j]), mu=rwkv_mu[j], w_r=bf(rwkv_w_r[j]), w_k=bf(rwkv_w_k[j]),
               w_v=bf(rwkv_w_v[j]), w_z=bf(rwkv_w_z[j]), w0=row(rwkv_w0[j]), w1=bf(rwkv_w1[j]),
               w2=bf(rwkv_w2[j]), a0=row(rwkv_a0[j]), a1=bf(rwkv_a1[j]), a2=bf(rwkv_a2[j]),
               k_k=row(rwkv_k_k[j]), k_a=row(rwkv_k_a[j]), r_k=row(rwkv_r_k[j]),
               gn_g=row(rwkv_gn_g[j]), gn_b=row(rwkv_gn_b[j]), w_o=bf(rwkv_w_o[j]), g=g, gt=gt)
          for j in range(n_rwkv)]
    vres = [(row(rwkv_v0[j]), bf(rwkv_v1[j]), bf(rwkv_v2[j])) for j in range(rwkv_v0.shape[0])]
    pw = [dict(norm=row(pool_norm[j]), w_in=bf(pool_w_in[j]), w_grp=bf(pool_w_grp[j]),
               b_grp=row(pool_b_grp[j]), scale=row(pool_scale[j]), w_o=bf(pool_w_o[j]))
          for j in range(n_pool)]
    fn = row(final_norm)

    nb = x_prompt.shape[0]
    shift0 = jnp.zeros((n_rwkv, nb, d), state_shift.dtype)
    wkv0 = jnp.zeros((n_rwkv, nb, n_heads, HEAD_DIM, HEAD_DIM), state_wkv.dtype)
    buf0 = jnp.zeros((n_pool, nb, state_pool.shape[2], c), state_pool.dtype)
    y_p, sh_p, wkv_p, pool_p = _trunk(x_prompt, shift0, wkv0, buf0, 0, rw, pw, vres, fn)
    y_s, sh_s, wkv_s, pool_s = _trunk(x_sample, state_shift, state_wkv, state_pool, PAST_LEN, rw, pw,
                                      vres, fn)
    return (y_p, y_s, sh_p, wkv_p, pool_p, sh_s, wkv_s, pool_s)
```

```python
import functools
import math

import jax
import jax.numpy as jnp
from jax import lax
from jax.experimental import pallas as pl
from jax.experimental.pallas import tpu as pltpu

F32 = jnp.float32
BF16 = jnp.bfloat16

HEAD_DIM = 64
POOL_WINDOWS = (2, 4, 8, 16)
POOL_HALO = 16
NORM_EPS = 1e-6
GN_EPS = 64e-5
PAST_LEN = 16384

VMEM_LIMIT_BYTES = 56 * 1024 * 1024
ROW_TILE = 128
OUT_ROW_TILE = 256
POOL_ROW_TILE = 256
HEAD_SUM_SLAB = 256
WKV_ROW_TILE = 64
WKV_SEQS_PER_STEP = 2
WKV_SHORT_SEQS_PER_STEP = 4
WKV_CHUNK = 32
WKV_HEADS_PER_GROUP = 4


def _bdot(a, b):
    return jnp.dot(a.astype(BF16), b.astype(BF16), preferred_element_type=F32)


def _bdot_nt(a, b):
    return lax.dot_general(a.astype(BF16), b.astype(BF16), (((1,), (1,)), ((), ())),
                           preferred_element_type=F32)


def _bdot_tn(a, b):
    return lax.dot_general(a.astype(BF16), b.astype(BF16), (((0,), (0,)), ((), ())),
                           preferred_element_type=F32)


def _split2(x):
    hi = x.astype(BF16)
    lo = (x - hi.astype(F32)).astype(BF16)
    return hi, lo


def _split3(x):
    hi = x.astype(BF16)
    r1 = x - hi.astype(F32)
    mid = r1.astype(BF16)
    lo = (r1 - mid.astype(F32)).astype(BF16)
    return hi, mid, lo


def _dot_exact_rhs(x, sel):
    hi, lo = _split2(x)
    return (jnp.dot(hi, sel, preferred_element_type=F32)
            + jnp.dot(lo, sel, preferred_element_type=F32))


def _head_sum(x, ones_ref):
    sel = ones_ref[...]
    b = sel.shape[0]
    return jnp.concatenate([_dot_exact_rhs(x[:, i:i + b], sel) for i in range(0, x.shape[1], b)], axis=1)


def _sum(xs):
    return functools.reduce(lambda a, b: a + b, xs)


def _rms_norm(x, g):
    return x * lax.rsqrt(jnp.mean(x * x, axis=-1, keepdims=True) + NORM_EPS) * g


def _sigmoid(x):
    return 1.0 / (1.0 + jnp.exp(-x))


def _const_spec(shape):
    nd = len(shape)
    return pl.BlockSpec(shape, lambda *_: (0,) * nd, pipeline_mode=pl.Buffered(1))


def _params(sem):
    return pltpu.CompilerParams(dimension_semantics=sem, vmem_limit_bytes=VMEM_LIMIT_BYTES)


def _rwkv_proj_kernel(*refs, seq_len, tile, long_mode, has_vres):
    it = iter(refs)
    x_ref = next(it)
    halo_ref = next(it)
    norm_ref, mu_ref, wr_ref, wk_ref, wv_ref, wz_ref = (next(it) for _ in range(6))
    w0_ref, w1_ref, w2_ref, a0_ref, a1_ref, a2_ref = (next(it) for _ in range(6))
    kk_ref, ka_ref, rk_ref, ones_ref = (next(it) for _ in range(4))
    if has_vres:
        v0_ref, v1_ref, v2_ref, vfirst_ref = (next(it) for _ in range(4))
    r_o, lw_o, kh_o, v_o, kkn_o, ba_o, bonus_o, sz_o, xn_o = (next(it) for _ in range(9))
    carry_ref = next(it) if long_mode else None

    x = x_ref[...]
    xn = _rms_norm(x, norm_ref[...])
    rolled = pltpu.roll(xn, 1, 0)
    row = lax.broadcasted_iota(jnp.int32, xn.shape, 0)
    if long_mode:
        t = pl.program_id(1)

        @pl.when(t == 0)
        def _():
            carry_ref[...] = jnp.broadcast_to(halo_ref[0], carry_ref.shape)

        x_prev = jnp.where(row == 0, carry_ref[7:8, :], rolled)
        carry_ref[...] = xn[tile - 8:tile, :]
        xn_o[0] = xn[tile - 1:tile, :]
    else:
        x_prev = jnp.where((row & (seq_len - 1)) == 0, halo_ref[...], rolled)
        xn_o[...] = xn
    dx = x_prev - xn
    mu = mu_ref[...]
    xr, xw, xk, xv, xa, xg = (xn + dx * mu[m:m + 1, :] for m in range(6))

    r = _bdot(xr, wr_ref[...])
    k = _bdot(xk, wk_ref[...])
    v = _bdot(xv, wv_ref[...])
    z = _bdot(xg, wz_ref[...])

    wl = w0_ref[...] + _bdot(jnp.tanh(_bdot(xw, w1_ref[...])), w2_ref[...])
    neg = -wl
    softplus = jnp.maximum(neg, 0.0) + jnp.log(1.0 + jnp.exp(-jnp.abs(neg)))
    lw_o[...] = -jnp.exp(-softplus - 0.5)

    a = _sigmoid(a0_ref[...] + _bdot(_bdot(xa, a1_ref[...]), a2_ref[...]))
    if has_vres:
        gate = _sigmoid(v0_ref[...] + _bdot(_bdot(xv, v1_ref[...]), v2_ref[...]))
        v = v + (vfirst_ref[...].astype(F32) - v) * gate

    kk = k * kk_ref[...]
    nrm = jnp.maximum(jnp.sqrt(_head_sum(kk * kk, ones_ref)), 1e-12)
    kk = kk / nrm
    kh = k * (1.0 + (a - 1.0) * ka_ref[...])
    bonus_o[...] = _head_sum(r * kh * rk_ref[...], ones_ref) * v
    r_o[...] = r.astype(r_o.dtype)
    kh_o[...] = kh.astype(kh_o.dtype)
    v_o[...] = v.astype(v_o.dtype)
    kkn_o[...] = kk.astype(kkn_o.dtype)
    ba_o[...] = (kk * a).astype(ba_o.dtype)
    sz_o[...] = z * _sigmoid(z)


def _rwkv_proj(x2d, halo, seq_len, p, vres, vfirst):
    rows, d = x2d.shape
    c = p['w_r'].shape[1]
    nb = rows // seq_len
    long_mode = seq_len >= ROW_TILE
    tile = ROW_TILE if long_mode else min(ROW_TILE, rows)
    has_vres = vres is not None
    if long_mode:
        tps = seq_len // tile
        grid = (nb, tps)
        row_map = lambda b, t: (b * tps + t, 0)
        halo_spec = pl.BlockSpec((1, 1, d), lambda b, t: (b, 0, 0))
        xn_shape = jax.ShapeDtypeStruct((nb, 1, d), F32)
        xn_spec = pl.BlockSpec((1, 1, d), lambda b, t: (b, 0, 0))
        scratch = [pltpu.VMEM((8, d), F32)]
        sem = ("arbitrary", "arbitrary")
    else:
        grid = (rows // tile,)
        row_map = lambda i: (i, 0)
        halo_spec = pl.BlockSpec((tile, d), row_map)
        xn_shape = jax.ShapeDtypeStruct((rows, d), F32)
        xn_spec = pl.BlockSpec((tile, d), row_map)
        scratch = []
        sem = ("arbitrary",)
    row_d = pl.BlockSpec((tile, d), row_map)
    row_c = pl.BlockSpec((tile, c), row_map)
    consts = [p['norm'], p['mu'], p['w_r'], p['w_k'], p['w_v'], p['w_z'], p['w0'], p['w1'], p['w2'],
              p['a0'], p['a1'], p['a2'], p['k_k'], p['k_a'], p['r_k'], p['ones']]
    args = [x2d, halo] + consts
    in_specs = [row_d, halo_spec] + [_const_spec(a.shape) for a in consts]
    if has_vres:
        args += list(vres) + [vfirst]
        in_specs += [_const_spec(a.shape) for a in vres] + [row_c]
    out_dtypes = [BF16, F32, BF16, BF16, BF16, BF16, F32, F32]
    out_shape = [jax.ShapeDtypeStruct((rows, c), dt) for dt in out_dtypes] + [xn_shape]
    out_specs = [row_c] * 8 + [xn_spec]
    return pl.pallas_call(
        functools.partial(_rwkv_proj_kernel, seq_len=seq_len, tile=tile, long_mode=long_mode,
                          has_vres=has_vres),
        grid=grid, in_specs=in_specs, out_specs=out_specs, out_shape=out_shape,
        scratch_shapes=scratch, compiler_params=_params(sem), name="rwkv_proj",
    )(*args)


def _wkv_masks(chunk, hpg, op_dtype):
    w = HEAD_DIM * hpg
    n = chunk * hpg
    sh = int(math.log2(chunk))
    lane = lax.broadcasted_iota(jnp.int32, (1, w), 1)
    op_masks = [((lane >> 6) == e).astype(op_dtype) for e in range(hpg)]
    t2 = lax.broadcasted_iota(jnp.int32, (chunk, n), 0)
    s2 = lax.broadcasted_iota(jnp.int32, (chunk, n), 1) & (chunk - 1)
    strict = (s2 < t2).astype(F32)
    incl = (s2 <= t2).astype(F32)
    eye_row = (s2 == t2).astype(F32)
    rb = lax.broadcasted_iota(jnp.int32, (n, n), 0)
    cb = lax.broadcasted_iota(jnp.int32, (n, n), 1)
    bmask = ((rb >> sh) == (cb >> sh)).astype(F32)
    return op_masks, strict, incl, eye_row, bmask


def _load_state(s_ref, seq, gi, hpg):
    return jnp.concatenate([s_ref[seq, gi * hpg + e] for e in range(hpg)], axis=1)


def _store_state(s_ref, seq, gi, hpg, s_cat):
    for e in range(hpg):
        s_ref[seq, gi * hpg + e] = s_cat[:, e * HEAD_DIM:(e + 1) * HEAD_DIM]


def _wkv_prepass(r_ref, lw_ref, k_ref, v_ref, kk_ref, ba_ref, ops, clast_ref, chunk):
    at_ref, rt_ref, bt_ref, kt_ref, bh_ref, kh_ref, vb_ref = ops
    nr, c_dim = at_ref.shape
    flat = lambda ref: ref[...].astype(F32).reshape(nr, c_dim)
    sh = int(math.log2(chunk))
    ti = lax.broadcasted_iota(jnp.int32, (nr, nr), 0)
    si = lax.broadcasted_iota(jnp.int32, (nr, nr), 1)
    same = (ti >> sh) == (si >> sh)
    tri = (same & (ti >= si)).astype(BF16)
    ones = same.astype(BF16)
    lw = flat(lw_ref)
    parts = _split3(lw)
    c = _sum([jnp.dot(tri, q, preferred_element_type=F32) for q in parts])
    c_last = _sum([jnp.dot(ones, q, preferred_element_type=F32) for q in parts])
    dt = at_ref.dtype
    kk = flat(kk_ref)
    ba = flat(ba_ref)
    k = flat(k_ref)
    at_ref[...] = (-kk * jnp.exp(c - lw)).astype(dt)
    rt_ref[...] = (flat(r_ref) * jnp.exp(c)).astype(dt)
    e_inv = jnp.exp(-c)
    bt_ref[...] = (ba * e_inv).astype(dt)
    kt_ref[...] = (k * e_inv).astype(dt)
    e_hat = jnp.exp(c_last - c)
    bh_ref[...] = (ba * e_hat).astype(dt)
    kh_ref[...] = (k * e_hat).astype(dt)
    vb_ref[...] = flat(v_ref).astype(dt)
    clast_ref[...] = c_last


def _wkv_chunk_step(states, row_starts, ops, clast_ref, masks, chunk, hpg, groups):
    at_ref, rt_ref, bt_ref, kt_ref, bh_ref, kh_ref, vb_ref = ops
    op_masks, strict, incl, eye_row, bmask = masks
    w = HEAD_DIM * hpg
    chains = [(s, g) for s in range(len(row_starts)) for g in range(groups)]

    def tile(ref, s, g):
        return ref[pl.ds(row_starts[s], chunk), g * w:(g + 1) * w]

    def stack(m):
        return jnp.concatenate([m.astype(lm.dtype) * lm for lm in op_masks], axis=0)

    def block_diag(m):
        return jnp.concatenate([m] * hpg, axis=0) * bmask

    lhs2 = [jnp.concatenate([tile(at_ref, s, g), tile(rt_ref, s, g)], axis=0) for s, g in chains]
    v_t = [tile(vb_ref, s, g) for s, g in chains]
    v_st = [stack(x) for x in v_t]
    sc_b = [_bdot_nt(l, stack(tile(bt_ref, s, g))) for l, (s, g) in zip(lhs2, chains)]
    sc_k = [_bdot_nt(l, stack(tile(kt_ref, s, g))) for l, (s, g) in zip(lhs2, chains)]
    sr = [_bdot_nt(l, stack(states[s][g])) for l, (s, g) in zip(lhs2, chains)]

    x_ab = [m[:chunk] * strict for m in sc_b]
    x_rb = [m[chunk:] * incl for m in sc_b]
    x_ak = [m[:chunk] * strict for m in sc_k]
    x_rk = [m[chunk:] * incl for m in sc_k]

    pw = [_bdot(a, block_diag(a)) for a in x_ab]
    xa = [m[:chunk] + _bdot(x, vs) for m, x, vs in zip(sr, x_ak, v_st)]
    y_k = [m[chunk:] + _bdot(x, vs) for m, x, vs in zip(sr, x_rk, v_st)]
    t_inv = [eye_row + a for a in x_ab]
    for _ in range(int(math.log2(chunk)) - 2):
        t_new = [t + _bdot(p, block_diag(t)) for t, p in zip(t_inv, pw)]
        pw = [_bdot(p, block_diag(p)) for p in pw]
        t_inv = t_new
    t_inv = [t + _bdot(p, block_diag(t)) for t, p in zip(t_inv, pw)]

    u = [_bdot(t, stack(x)) for t, x in zip(t_inv, xa)]
    y = [yk + _bdot(x, stack(uu)) for yk, x, uu in zip(y_k, x_rb, u)]
    upd = []
    for uu, vv, (s, g) in zip(u, v_t, chains):
        uv = jnp.concatenate([uu, vv.astype(F32)], axis=0)
        uv_heads = jnp.concatenate([uv[:, e * HEAD_DIM:(e + 1) * HEAD_DIM] for e in range(hpg)], axis=0)
        bk = stack(jnp.concatenate([tile(bh_ref, s, g), tile(kh_ref, s, g)], axis=0))
        upd.append(_bdot_tn(uv_heads, bk))

    new_states = [[None] * groups for _ in row_starts]
    ys = [[None] * groups for _ in row_starts]
    for (s, g), up, yy in zip(chains, upd, y):
        decay = jnp.exp(clast_ref[pl.ds(row_starts[s], 1), g * w:(g + 1) * w])
        new_states[s][g] = states[s][g] * decay + up
        ys[s][g] = yy
    return new_states, ys


def _wkv_kernel(r_ref, lw_ref, k_ref, v_ref, kk_ref, ba_ref, s_in_ref, y_ref, s_out_ref, *scratch,
                chunk, hpg, groups, nseq, rows, long_mode):
    ops = scratch[:7]
    clast_ref = scratch[7]
    w = HEAD_DIM * hpg
    _wkv_prepass(r_ref, lw_ref, k_ref, v_ref, kk_ref, ba_ref, ops, clast_ref, chunk)
    masks = _wkv_masks(chunk, hpg, ops[0].dtype)

    if long_mode:
        state_ref = scratch[8]
        t = pl.program_id(1)

        @pl.when(t == 0)
        def _():
            for s in range(nseq):
                for g in range(groups):
                    state_ref[s, g] = _load_state(s_in_ref, s, g, hpg)

        def body(ci, carry):
            starts = [pl.multiple_of(s * rows + ci * chunk, chunk) for s in range(nseq)]
            states = [[state_ref[s, g] for g in range(groups)] for s in range(nseq)]
            new_states, ys = _wkv_chunk_step(states, starts, ops, clast_ref, masks, chunk, hpg, groups)
            for s in range(nseq):
                for g in range(groups):
                    state_ref[s, g] = new_states[s][g]
                    y_ref[s, pl.ds(pl.multiple_of(ci * chunk, chunk), chunk), g * w:(g + 1) * w] = ys[s][g]
            return carry

        lax.fori_loop(0, rows // chunk, body, 0)

        @pl.when(t == pl.num_programs(1) - 1)
        def _():
            for s in range(nseq):
                for g in range(groups):
                    _store_state(s_out_ref, s, g, hpg, state_ref[s, g])
    else:
        starts = [s * rows for s in range(nseq)]
        states = [[_load_state(s_in_ref, s, g, hpg) for g in range(groups)] for s in range(nseq)]
        new_states, ys = _wkv_chunk_step(states, starts, ops, clast_ref, masks, chunk, hpg, groups)
        for s in range(nseq):
            for g in range(groups):
                _store_state(s_out_ref, s, g, hpg, new_states[s][g])
                y_ref[s, :, g * w:(g + 1) * w] = ys[s][g]


def _wkv(r, lw, kh, v, kk, ba, s_in, seq_len):
    total, c = r.shape
    nb = total // seq_len
    hpg = WKV_HEADS_PER_GROUP
    groups = c // (HEAD_DIM * hpg)
    n_heads = c // HEAD_DIM
    long_mode = seq_len > WKV_CHUNK
    if long_mode:
        chunk, nseq, rows = WKV_CHUNK, WKV_SEQS_PER_STEP, min(WKV_ROW_TILE, seq_len)
        tps = seq_len // rows
        grid = (nb // nseq, tps)
        row_spec = pl.BlockSpec((nseq, None, rows, c), lambda b, t: (b, t, 0, 0))
        s_spec = pl.BlockSpec((nseq, n_heads, HEAD_DIM, HEAD_DIM), lambda b, t: (b, 0, 0, 0))
        shape4 = (nb, tps, rows, c)
        op_dtype = BF16
        sem = ("arbitrary", "arbitrary")
    else:
        chunk, nseq, rows = seq_len, WKV_SHORT_SEQS_PER_STEP, seq_len
        grid = (nb // nseq,)
        row_spec = pl.BlockSpec((nseq, None, rows, c), lambda b: (b, 0, 0, 0))
        s_spec = pl.BlockSpec((nseq, n_heads, HEAD_DIM, HEAD_DIM), lambda b: (b, 0, 0, 0))
        shape4 = (nb, 1, rows, c)
        op_dtype = F32
        sem = ("arbitrary",)
    scratch = [pltpu.VMEM((nseq * rows, c), op_dtype)] * 7 + [pltpu.VMEM((nseq * rows, c), F32)]
    if long_mode:
        scratch.append(pltpu.VMEM((nseq, groups, HEAD_DIM, HEAD_DIM * hpg), F32))
    ins = [a.reshape(shape4) for a in (r, lw, kh, v, kk, ba)]
    y, s_out = pl.pallas_call(
        functools.partial(_wkv_kernel, chunk=chunk, hpg=hpg, groups=groups, nseq=nseq, rows=rows,
                          long_mode=long_mode),
        grid=grid, in_specs=[row_spec] * 6 + [s_spec], out_specs=[row_spec, s_spec],
        out_shape=[jax.ShapeDtypeStruct(shape4, F32), jax.ShapeDtypeStruct(s_in.shape, F32)],
        scratch_shapes=scratch, compiler_params=_params(sem), name="wkv7_chunked",
    )(*ins, s_in)
    return y.reshape(total, c), s_out


def _rwkv_out_kernel(y_ref, bonus_ref, sz_ref, x_ref, gng_ref, gnb_ref, wo_ref, ones_ref, o_ref):
    y = y_ref[...]
    inv_n = 1.0 / HEAD_DIM
    mean = _head_sum(y, ones_ref) * inv_n
    d = y - mean
    var = _head_sum(d * d, ones_ref) * inv_n
    yn = d * lax.rsqrt(var + GN_EPS) * gng_ref[...] + gnb_ref[...]
    gated = (yn + bonus_ref[...]) * sz_ref[...]
    o_ref[...] = x_ref[...] + _bdot(gated, wo_ref[...])


def _rwkv_out(y, bonus, sz, x2d, p):
    rows, c = y.shape
    d = x2d.shape[1]
    tile = min(OUT_ROW_TILE, rows)
    row_c = pl.BlockSpec((tile, c), lambda i: (i, 0))
    row_d = pl.BlockSpec((tile, d), lambda i: (i, 0))
    consts = [p['gn_g'], p['gn_b'], p['w_o'], p['ones']]
    return pl.pallas_call(
        _rwkv_out_kernel, grid=(rows // tile,),
        in_specs=[row_c, row_c, row_c, row_d] + [_const_spec(a.shape) for a in consts],
        out_specs=row_d, out_shape=jax.ShapeDtypeStruct((rows, d), F32),
        compiler_params=_params(("arbitrary",)), name="rwkv_out",
    )(y, bonus, sz, x2d, *consts)


def _pool_kernel(*refs, seq_len, tile, long_mode, pos0, final):
    it = iter(refs)
    x_ref, buf_ref, norm_ref, win_ref, wgrp_ref, bgrp_ref, scale_ref, wo_ref = (next(it) for _ in range(8))
    fnorm_ref = next(it) if final else None
    o_ref, nbuf_ref = next(it), next(it)
    u_ref, ext_ref, p_ref = next(it), next(it), next(it)
    carry_ref = next(it) if long_mode else None

    c = u_ref.shape[1]
    pg = c // len(POOL_WINDOWS)
    tl = tile if long_mode else seq_len
    x = x_ref[...]
    xn = _rms_norm(x, norm_ref[...])
    uz = _bdot(xn, win_ref[...])
    u_ref[...] = uz[:, :c]
    z = uz[:, c:]

    if long_mode:
        t = pl.program_id(1)
        row0 = t * tile
    else:
        row0 = 0
    pos = pos0 + row0 + lax.broadcasted_iota(jnp.int32, (tl, 1), 0)

    def pool_one(base, halo):
        ext_ref[0:POOL_HALO, :] = halo
        ext_ref[POOL_HALO:POOL_HALO + tl, :] = u_ref[pl.ds(base, tl), :]
        for g, win in enumerate(POOL_WINDOWS):
            lanes = slice(g * pg, (g + 1) * pg)
            s = ext_ref[:, lanes]
            dlt = 1
            while dlt < win:
                s = s + pltpu.roll(s, dlt, 0)
                dlt *= 2
            cnt = jnp.minimum(win, pos + 1).astype(F32)
            cur = ext_ref[POOL_HALO:POOL_HALO + tl, lanes]
            p_ref[pl.ds(base, tl), lanes] = s[POOL_HALO:POOL_HALO + tl, :] / cnt - cur
        return ext_ref[tl:tl + POOL_HALO, :]

    if long_mode:
        @pl.when(t == 0)
        def _():
            carry_ref[...] = buf_ref[0]

        tail = pool_one(0, carry_ref[...])
        carry_ref[...] = tail
        nbuf_ref[0] = tail
    else:
        def body(s, carry):
            base = pl.multiple_of(s * seq_len, seq_len)
            nbuf_ref[s] = pool_one(base, buf_ref[s])
            return carry

        lax.fori_loop(0, tile // seq_len, body, 0)

    p = p_ref[...]
    mixed = jnp.concatenate(
        [_bdot(p[:, g * pg:(g + 1) * pg], wgrp_ref[g]) for g in range(len(POOL_WINDOWS))], axis=1)
    mixed = (mixed + bgrp_ref[...]) * scale_ref[...]
    gated = mixed * (z * _sigmoid(z))
    out = x + _bdot(gated, wo_ref[...])
    if final:
        out = _rms_norm(out, fnorm_ref[...])
    o_ref[...] = out


def _pool_layer(x2d, buf16, seq_len, pos0, p, final_norm):
    rows, d = x2d.shape
    c = p['w_o'].shape[0]
    nb = rows // seq_len
    long_mode = seq_len >= POOL_ROW_TILE
    final = final_norm is not None
    if long_mode:
        tile = POOL_ROW_TILE
        tps = seq_len // tile
        grid = (nb, tps)
        row_spec = pl.BlockSpec((tile, d), lambda b, t: (b * tps + t, 0))
        buf_spec = pl.BlockSpec((1, POOL_HALO, c), lambda b, t: (b, 0, 0))
        tl = tile
        sem = ("arbitrary", "arbitrary")
    else:
        tile = min(POOL_ROW_TILE, rows)
        grid = (rows // tile,)
        row_spec = pl.BlockSpec((tile, d), lambda i: (i, 0))
        buf_spec = pl.BlockSpec((tile // seq_len, POOL_HALO, c), lambda i: (i, 0, 0))
        tl = seq_len
        sem = ("arbitrary",)
    consts = [p['norm'], p['w_in'], p['w_grp'], p['b_grp'], p['scale'], p['w_o']]
    if final:
        consts.append(final_norm)
    scratch = [pltpu.VMEM((tile, c), F32), pltpu.VMEM((POOL_HALO + tl, c), F32), pltpu.VMEM((tile, c), F32)]
    if long_mode:
        scratch.append(pltpu.VMEM((POOL_HALO, c), F32))
    return pl.pallas_call(
        functools.partial(_pool_kernel, seq_len=seq_len, tile=tile, long_mode=long_mode, pos0=pos0,
                          final=final),
        grid=grid, in_specs=[row_spec, buf_spec] + [_const_spec(a.shape) for a in consts],
        out_specs=[row_spec, buf_spec],
        out_shape=[jax.ShapeDtypeStruct((rows, d), F32), jax.ShapeDtypeStruct((nb, POOL_HALO, c), F32)],
        scratch_shapes=scratch, compiler_params=_params(sem), name="pool_mixer",
    )(x2d, buf16, *consts)


def _trunk(x, shift0, wkv0, buf0, pos0, rw, pw, vres, final_norm):
    nb, seq_len, d = x.shape
    x2d = x.reshape(nb * seq_len, d)
    new_shift, new_wkv, new_buf = [], [], []
    v_first = None
    depth = len(rw) + len(pw)
    for i in range(depth):
        j = i // 2
        if i % 2 == 0:
            p = rw[j]
            if seq_len >= ROW_TILE:
                halo = shift0[j][:, None, :]
            else:
                halo = jnp.repeat(shift0[j], seq_len, axis=0)
            r, lw, kh, v, kk, ba, bonus, sz, xn_last = _rwkv_proj(
                x2d, halo, seq_len, p, None if j == 0 else vres[j - 1], v_first)
            if j == 0:
                v_first = v
            y, s_new = _wkv(r, lw, kh, v, kk, ba, wkv0[j], seq_len)
            x2d = _rwkv_out(y, bonus, sz, x2d, p)
            new_shift.append(xn_last.reshape(nb, -1, d)[:, -1, :])
            new_wkv.append(s_new)
        else:
            p = pw[j]
            buf16 = jnp.pad(buf0[j], ((0, 0), (1, 0), (0, 0)))
            x2d, nbuf = _pool_layer(x2d, buf16, seq_len, pos0, p,
                                    final_norm if i == depth - 1 else None)
            new_buf.append(nbuf[:, 1:, :])
    return (x2d.reshape(nb, seq_len, d), jnp.stack(new_shift), jnp.stack(new_wkv), jnp.stack(new_buf))


def kernel(x_prompt, x_sample, state_shift, state_wkv, state_pool, rwkv_norm, rwkv_mu, rwkv_w_r, rwkv_w_k, rwkv_w_v, rwkv_w_z, rwkv_w0, rwkv_w1, rwkv_w2, rwkv_a0, rwkv_a1, rwkv_a2, rwkv_v0, rwkv_v1, rwkv_v2, rwkv_k_k, rwkv_k_a, rwkv_r_k, rwkv_gn_g, rwkv_gn_b, rwkv_w_o, pool_norm, pool_w_in, pool_w_grp, pool_b_grp, pool_scale, pool_w_o, final_norm):
    n_rwkv, d, c = rwkv_w_r.shape
    n_pool = pool_w_in.shape[0]
    n_heads = c // HEAD_DIM
    bf = lambda a: a.astype(BF16)
    row = lambda a: a.reshape(1, -1)
    slab = min(c, HEAD_SUM_SLAB)
    head_of_lane = jnp.arange(slab, dtype=jnp.int32) // HEAD_DIM
    ones = (head_of_lane[:, None] == head_of_lane[None, :]).astype(BF16)
    rw = [dict(norm=row(rwkv_norm[j]), mu=rwkv_mu[j], w_r=bf(rwkv_w_r[j]), w_k=bf(rwkv_w_k[j]),
               w_v=bf(rwkv_w_v[j]), w_z=bf(rwkv_w_z[j]), w0=row(rwkv_w0[j]), w1=bf(rwkv_w1[j]),
               w2=bf(rwkv_w2[j]), a0=row(rwkv_a0[j]), a1=bf(rwkv_a1[j]), a2=bf(rwkv_a2[j]),
               k_k=row(rwkv_k_k[j]), k_a=row(rwkv_k_a[j]), r_k=row(rwkv_r_k[j]),
               gn_g=row(rwkv_gn_g[j]), gn_b=row(rwkv_gn_b[j]), w_o=bf(rwkv_w_o[j]), ones=ones)
          for j in range(n_rwkv)]
    vres = [(row(rwkv_v0[j]), bf(rwkv_v1[j]), bf(rwkv_v2[j])) for j in range(rwkv_v0.shape[0])]
    pw = [dict(norm=row(pool_norm[j]), w_in=bf(pool_w_in[j]), w_grp=bf(pool_w_grp[j]),
               b_grp=row(pool_b_grp[j]), scale=row(pool_scale[j]), w_o=bf(pool_w_o[j]))
          for j in range(n_pool)]
    fn = row(final_norm)

    nb = x_prompt.shape[0]
    shift0 = jnp.zeros((n_rwkv, nb, d), state_shift.dtype)
    wkv0 = jnp.zeros((n_rwkv, nb, n_heads, HEAD_DIM, HEAD_DIM), state_wkv.dtype)
    buf0 = jnp.zeros((n_pool, nb, state_pool.shape[2], c), state_pool.dtype)
    y_p, sh_p, wkv_p, pool_p = _trunk(x_prompt, shift0, wkv0, buf0, 0, rw, pw, vres, fn)
    y_s, sh_s, wkv_s, pool_s = _trunk(x_sample, state_shift, state_wkv, state_pool, PAST_LEN, rw, pw,
                                      vres, fn)
    return (y_p, y_s, sh_p, wkv_p, pool_p, sh_s, wkv_s, pool_s)
```

```python
import functools
import math

import jax
import jax.numpy as jnp
from jax import lax
from jax.experimental import pallas as pl
from jax.experimental.pallas import tpu as pltpu

F32 = jnp.float32
BF16 = jnp.bfloat16

HEAD_DIM = 64
POOL_WINDOWS = (2, 4, 8, 16)
POOL_HALO = 16
NORM_EPS = 1e-6
GN_EPS = 64e-5
PAST_LEN = 16384

VMEM_LIMIT_BYTES = 56 * 1024 * 1024
ROW_TILE = 256
PROJ_SLAB = 512
OUT_ROW_TILE = 256
POOL_ROW_TILE = 256
HEAD_SUM_SLAB = 256
WKV_ROW_TILE = 64
WKV_SEQS_PER_STEP = 2
WKV_SHORT_SEQS_PER_STEP = 4
WKV_CHUNK = 32
WKV_HEADS_PER_GROUP = 4


def _bdot(a, b):
    return jnp.dot(a.astype(BF16), b.astype(BF16), preferred_element_type=F32)


def _bdot_nt(a, b):
    return lax.dot_general(a.astype(BF16), b.astype(BF16), (((1,), (1,)), ((), ())),
                           preferred_element_type=F32)


def _bdot_tn(a, b):
    return lax.dot_general(a.astype(BF16), b.astype(BF16), (((0,), (0,)), ((), ())),
                           preferred_element_type=F32)


def _split2(x):
    hi = x.astype(BF16)
    lo = (x - hi.astype(F32)).astype(BF16)
    return hi, lo


def _head_sum(x, ones_ref):
    sel = ones_ref[...]
    b = sel.shape[0]
    xb = x.astype(BF16)
    return jnp.concatenate([jnp.dot(xb[:, i:i + b], sel, preferred_element_type=F32)
                            for i in range(0, x.shape[1], b)], axis=1)


def _sum(xs):
    return functools.reduce(lambda a, b: a + b, xs)


def _rms_norm(x, g):
    return x * lax.rsqrt(jnp.mean(x * x, axis=-1, keepdims=True) + NORM_EPS) * g


def _sigmoid(x):
    return 1.0 / (1.0 + jnp.exp(-x))


def _const_spec(shape):
    nd = len(shape)
    return pl.BlockSpec(shape, lambda *_: (0,) * nd, pipeline_mode=pl.Buffered(1))


def _params(sem):
    return pltpu.CompilerParams(dimension_semantics=sem, vmem_limit_bytes=VMEM_LIMIT_BYTES)


def _rwkv_proj_kernel(*refs, seq_len, tile, long_mode, has_vres):
    it = iter(refs)
    x_ref = next(it)
    halo_ref = next(it)
    norm_ref, mu_ref, wr_ref, wk_ref, wv_ref, wz_ref = (next(it) for _ in range(6))
    w0_ref, w1_ref, w2_ref, a0_ref, a1_ref, a2_ref = (next(it) for _ in range(6))
    kk_ref, ka_ref, rk_ref, ones_ref = (next(it) for _ in range(4))
    if has_vres:
        v0_ref, v1_ref, v2_ref, vfirst_ref = (next(it) for _ in range(4))
    r_o, lw_o, kh_o, v_o, kkn_o, ba_o, bonus_o, sz_o, xn_o = (next(it) for _ in range(9))
    carry_ref = next(it) if long_mode else None

    x = x_ref[...]
    xn = _rms_norm(x, norm_ref[...])
    rolled = pltpu.roll(xn, 1, 0)
    row = lax.broadcasted_iota(jnp.int32, xn.shape, 0)
    if long_mode:
        t = pl.program_id(1)

        @pl.when(t == 0)
        def _():
            carry_ref[...] = jnp.broadcast_to(halo_ref[0], carry_ref.shape)

        x_prev = jnp.where(row == 0, carry_ref[7:8, :], rolled)
        carry_ref[...] = xn[tile - 8:tile, :]
        xn_o[0] = xn[tile - 1:tile, :]
    else:
        x_prev = jnp.where((row & (seq_len - 1)) == 0, halo_ref[...], rolled)
        xn_o[...] = xn
    dx = x_prev - xn
    mu = mu_ref[...]
    xr, xw, xk, xv, xa, xg = ((xn + dx * mu[m:m + 1, :]).astype(BF16) for m in range(6))
    lora_w = jnp.tanh(_bdot(xw, w1_ref[...]))
    lora_a = _bdot(xa, a1_ref[...])
    if has_vres:
        lora_v = _bdot(xv, v1_ref[...])

    c = wr_ref.shape[1]
    slab = min(c, PROJ_SLAB)
    for j in range(0, c, slab):
        sl = slice(j, j + slab)
        r = _bdot(xr, wr_ref[:, sl])
        k = _bdot(xk, wk_ref[:, sl])
        v = _bdot(xv, wv_ref[:, sl])
        z = _bdot(xg, wz_ref[:, sl])

        neg = -(w0_ref[:, sl] + _bdot(lora_w, w2_ref[:, sl]))
        softplus = jnp.maximum(neg, 0.0) + jnp.log(1.0 + jnp.exp(-jnp.abs(neg)))
        lw_o[:, sl] = -jnp.exp(-softplus - 0.5)

        a = _sigmoid(a0_ref[:, sl] + _bdot(lora_a, a2_ref[:, sl]))
        if has_vres:
            gate = _sigmoid(v0_ref[:, sl] + _bdot(lora_v, v2_ref[:, sl]))
            v = v + (vfirst_ref[:, sl].astype(F32) - v) * gate

        kk = k * kk_ref[:, sl]
        nrm = jnp.maximum(jnp.sqrt(_head_sum(kk * kk, ones_ref)), 1e-12)
        kk = kk / nrm
        kh = k * (1.0 + (a - 1.0) * ka_ref[:, sl])
        bonus_o[:, sl] = _head_sum(r * kh * rk_ref[:, sl], ones_ref) * v
        r_o[:, sl] = r.astype(r_o.dtype)
        kh_o[:, sl] = kh.astype(kh_o.dtype)
        v_o[:, sl] = v.astype(v_o.dtype)
        kkn_o[:, sl] = kk.astype(kkn_o.dtype)
        ba_o[:, sl] = (kk * a).astype(ba_o.dtype)
        sz_o[:, sl] = z * _sigmoid(z)


def _rwkv_proj(x2d, halo, seq_len, p, vres, vfirst):
    rows, d = x2d.shape
    c = p['w_r'].shape[1]
    nb = rows // seq_len
    long_mode = seq_len >= ROW_TILE
    tile = ROW_TILE if long_mode else min(ROW_TILE, rows)
    has_vres = vres is not None
    if long_mode:
        tps = seq_len // tile
        grid = (nb, tps)
        row_map = lambda b, t: (b * tps + t, 0)
        halo_spec = pl.BlockSpec((1, 1, d), lambda b, t: (b, 0, 0))
        xn_shape = jax.ShapeDtypeStruct((nb, 1, d), F32)
        xn_spec = pl.BlockSpec((1, 1, d), lambda b, t: (b, 0, 0))
        scratch = [pltpu.VMEM((8, d), F32)]
        sem = ("arbitrary", "arbitrary")
    else:
        grid = (rows // tile,)
        row_map = lambda i: (i, 0)
        halo_spec = pl.BlockSpec((tile, d), row_map)
        xn_shape = jax.ShapeDtypeStruct((rows, d), F32)
        xn_spec = pl.BlockSpec((tile, d), row_map)
        scratch = []
        sem = ("arbitrary",)
    row_d = pl.BlockSpec((tile, d), row_map)
    row_c = pl.BlockSpec((tile, c), row_map)
    consts = [p['norm'], p['mu'], p['w_r'], p['w_k'], p['w_v'], p['w_z'], p['w0'], p['w1'], p['w2'],
              p['a0'], p['a1'], p['a2'], p['k_k'], p['k_a'], p['r_k'], p['ones']]
    args = [x2d, halo] + consts
    in_specs = [row_d, halo_spec] + [_const_spec(a.shape) for a in consts]
    if has_vres:
        args += list(vres) + [vfirst]
        in_specs += [_const_spec(a.shape) for a in vres] + [row_c]
    out_dtypes = [BF16, F32, BF16, BF16, BF16, BF16, F32, F32]
    out_shape = [jax.ShapeDtypeStruct((rows, c), dt) for dt in out_dtypes] + [xn_shape]
    out_specs = [row_c] * 8 + [xn_spec]
    return pl.pallas_call(
        functools.partial(_rwkv_proj_kernel, seq_len=seq_len, tile=tile, long_mode=long_mode,
                          has_vres=has_vres),
        grid=grid, in_specs=in_specs, out_specs=out_specs, out_shape=out_shape,
        scratch_shapes=scratch, compiler_params=_params(sem), name="rwkv_proj",
    )(*args)


def _wkv_masks(chunk, hpg, op_dtype):
    w = HEAD_DIM * hpg
    n = chunk * hpg
    sh = int(math.log2(chunk))
    lane = lax.broadcasted_iota(jnp.int32, (1, w), 1)
    op_masks = [((lane >> 6) == e).astype(op_dtype) for e in range(hpg)]
    t2 = lax.broadcasted_iota(jnp.int32, (chunk, n), 0)
    s2 = lax.broadcasted_iota(jnp.int32, (chunk, n), 1) & (chunk - 1)
    strict = (s2 < t2).astype(F32)
    incl = (s2 <= t2).astype(F32)
    eye_row = (s2 == t2).astype(F32)
    rb = lax.broadcasted_iota(jnp.int32, (n, n), 0)
    cb = lax.broadcasted_iota(jnp.int32, (n, n), 1)
    bmask = ((rb >> sh) == (cb >> sh)).astype(F32)
    return op_masks, strict, incl, eye_row, bmask


def _load_state(s_ref, seq, gi, hpg):
    return jnp.concatenate([s_ref[seq, gi * hpg + e] for e in range(hpg)], axis=1)


def _store_state(s_ref, seq, gi, hpg, s_cat):
    for e in range(hpg):
        s_ref[seq, gi * hpg + e] = s_cat[:, e * HEAD_DIM:(e + 1) * HEAD_DIM]


def _wkv_prepass(r_ref, lw_ref, k_ref, v_ref, kk_ref, ba_ref, ops, clast_ref, chunk):
    at_ref, rt_ref, bt_ref, kt_ref, bh_ref, kh_ref, vb_ref = ops
    nr, c_dim = at_ref.shape
    flat = lambda ref: ref[...].astype(F32).reshape(nr, c_dim)
    sh = int(math.log2(chunk))
    ti = lax.broadcasted_iota(jnp.int32, (nr, nr), 0)
    si = lax.broadcasted_iota(jnp.int32, (nr, nr), 1)
    tri = (((ti >> sh) == (si >> sh)) & (ti >= si)).astype(BF16)
    lw = flat(lw_ref)
    c = _sum([jnp.dot(tri, q, preferred_element_type=F32) for q in _split2(lw)])
    c_last = jnp.concatenate(
        [jnp.broadcast_to(c[q + chunk - 1:q + chunk, :], (chunk, c_dim)) for q in range(0, nr, chunk)], axis=0)
    dt = at_ref.dtype
    kk = flat(kk_ref)
    ba = flat(ba_ref)
    k = flat(k_ref)
    at_ref[...] = (-kk * jnp.exp(c - lw)).astype(dt)
    rt_ref[...] = (flat(r_ref) * jnp.exp(c)).astype(dt)
    e_inv = jnp.exp(-c)
    bt_ref[...] = (ba * e_inv).astype(dt)
    kt_ref[...] = (k * e_inv).astype(dt)
    e_hat = jnp.exp(c_last - c)
    bh_ref[...] = (ba * e_hat).astype(dt)
    kh_ref[...] = (k * e_hat).astype(dt)
    vb_ref[...] = flat(v_ref).astype(dt)
    clast_ref[...] = c_last


def _wkv_chunk_step(states, row_starts, ops, clast_ref, masks, chunk, hpg, groups):
    at_ref, rt_ref, bt_ref, kt_ref, bh_ref, kh_ref, vb_ref = ops
    op_masks, strict, incl, eye_row, bmask = masks
    low2 = jnp.concatenate([strict, incl], axis=0)
    w = HEAD_DIM * hpg
    chains = [(s, g) for s in range(len(row_starts)) for g in range(groups)]

    def tile(ref, s, g):
        return ref[pl.ds(row_starts[s], chunk), g * w:(g + 1) * w]

    def stack(m):
        return jnp.concatenate([m.astype(lm.dtype) * lm for lm in op_masks], axis=0)

    def block_diag(m):
        return jnp.concatenate([m] * hpg, axis=0) * bmask

    lhs2 = [jnp.concatenate([tile(at_ref, s, g), tile(rt_ref, s, g)], axis=0) for s, g in chains]
    v_t = [tile(vb_ref, s, g) for s, g in chains]
    v_st = [stack(x) for x in v_t]
    n = hpg * chunk
    sc = [_bdot_nt(l, jnp.concatenate([stack(tile(bt_ref, s, g)), stack(tile(kt_ref, s, g))], axis=0))
          for l, (s, g) in zip(lhs2, chains)]
    sr = [_bdot_nt(l, stack(states[s][g])) for l, (s, g) in zip(lhs2, chains)]

    x_ab = [m[:chunk, :n] * strict for m in sc]
    x_rb = [m[chunk:, :n] * incl for m in sc]

    pw = [_bdot(a, block_diag(a)) for a in x_ab]
    srk = [m + _bdot(q[:, n:] * low2, vs) for m, q, vs in zip(sr, sc, v_st)]
    xa = [m[:chunk] for m in srk]
    y_k = [m[chunk:] for m in srk]
    t_inv = [eye_row + a for a in x_ab]
    for _ in range(int(math.log2(chunk)) - 2):
        t_new = [t + _bdot(p, block_diag(t)) for t, p in zip(t_inv, pw)]
        pw = [_bdot(p, block_diag(p)) for p in pw]
        t_inv = t_new
    t_inv = [t + _bdot(p, block_diag(t)) for t, p in zip(t_inv, pw)]

    u = [_bdot(t, stack(x)) for t, x in zip(t_inv, xa)]
    y = [yk + _bdot(x, stack(uu)) for yk, x, uu in zip(y_k, x_rb, u)]
    upd = []
    for uu, vv, (s, g) in zip(u, v_t, chains):
        uv = jnp.concatenate([uu, vv.astype(F32)], axis=0)
        uv_heads = jnp.concatenate([uv[:, e * HEAD_DIM:(e + 1) * HEAD_DIM] for e in range(hpg)], axis=0)
        bk = stack(jnp.concatenate([tile(bh_ref, s, g), tile(kh_ref, s, g)], axis=0))
        upd.append(_bdot_tn(uv_heads, bk))

    new_states = [[None] * groups for _ in row_starts]
    ys = [[None] * groups for _ in row_starts]
    for (s, g), up, yy in zip(chains, upd, y):
        decay = jnp.exp(clast_ref[pl.ds(row_starts[s], 1), g * w:(g + 1) * w])
        new_states[s][g] = states[s][g] * decay + up
        ys[s][g] = yy
    return new_states, ys


def _wkv_kernel(r_ref, lw_ref, k_ref, v_ref, kk_ref, ba_ref, s_in_ref, y_ref, s_out_ref, *scratch,
                chunk, hpg, groups, nseq, rows, long_mode):
    ops = scratch[:7]
    clast_ref = scratch[7]
    w = HEAD_DIM * hpg
    _wkv_prepass(r_ref, lw_ref, k_ref, v_ref, kk_ref, ba_ref, ops, clast_ref, chunk)
    masks = _wkv_masks(chunk, hpg, ops[0].dtype)

    if long_mode:
        state_ref = scratch[8]
        t = pl.program_id(1)

        @pl.when(t == 0)
        def _():
            for s in range(nseq):
                for g in range(groups):
                    state_ref[s, g] = _load_state(s_in_ref, s, g, hpg)

        def body(ci, carry):
            starts = [pl.multiple_of(s * rows + ci * chunk, chunk) for s in range(nseq)]
            states = [[state_ref[s, g] for g in range(groups)] for s in range(nseq)]
            new_states, ys = _wkv_chunk_step(states, starts, ops, clast_ref, masks, chunk, hpg, groups)
            for s in range(nseq):
                for g in range(groups):
                    state_ref[s, g] = new_states[s][g]
                    y_ref[s, pl.ds(pl.multiple_of(ci * chunk, chunk), chunk), g * w:(g + 1) * w] = ys[s][g]
            return carry

        lax.fori_loop(0, rows // chunk, body, 0)

        @pl.when(t == pl.num_programs(1) - 1)
        def _():
            for s in range(nseq):
                for g in range(groups):
                    _store_state(s_out_ref, s, g, hpg, state_ref[s, g])
    else:
        starts = [s * rows for s in range(nseq)]
        states = [[_load_state(s_in_ref, s, g, hpg) for g in range(groups)] for s in range(nseq)]
        new_states, ys = _wkv_chunk_step(states, starts, ops, clast_ref, masks, chunk, hpg, groups)
        for s in range(nseq):
            for g in range(groups):
                _store_state(s_out_ref, s, g, hpg, new_states[s][g])
                y_ref[s, :, g * w:(g + 1) * w] = ys[s][g]


def _wkv(r, lw, kh, v, kk, ba, s_in, seq_len):
    total, c = r.shape
    nb = total // seq_len
    hpg = WKV_HEADS_PER_GROUP
    groups = c // (HEAD_DIM * hpg)
    n_heads = c // HEAD_DIM
    long_mode = seq_len > WKV_CHUNK
    if long_mode:
        chunk, nseq, rows = WKV_CHUNK, WKV_SEQS_PER_STEP, min(WKV_ROW_TILE, seq_len)
        tps = seq_len // rows
        grid = (nb // nseq, tps)
        row_spec = pl.BlockSpec((nseq, None, rows, c), lambda b, t: (b, t, 0, 0))
        s_spec = pl.BlockSpec((nseq, n_heads, HEAD_DIM, HEAD_DIM), lambda b, t: (b, 0, 0, 0))
        shape4 = (nb, tps, rows, c)
        op_dtype = BF16
        sem = ("arbitrary", "arbitrary")
    else:
        chunk, nseq, rows = seq_len, WKV_SHORT_SEQS_PER_STEP, seq_len
        grid = (nb // nseq,)
        row_spec = pl.BlockSpec((nseq, None, rows, c), lambda b: (b, 0, 0, 0))
        s_spec = pl.BlockSpec((nseq, n_heads, HEAD_DIM, HEAD_DIM), lambda b: (b, 0, 0, 0))
        shape4 = (nb, 1, rows, c)
        op_dtype = F32
        sem = ("arbitrary",)
    scratch = [pltpu.VMEM((nseq * rows, c), op_dtype)] * 7 + [pltpu.VMEM((nseq * rows, c), F32)]
    if long_mode:
        scratch.append(pltpu.VMEM((nseq, groups, HEAD_DIM, HEAD_DIM * hpg), F32))
    ins = [a.reshape(shape4) for a in (r, lw, kh, v, kk, ba)]
    y, s_out = pl.pallas_call(
        functools.partial(_wkv_kernel, chunk=chunk, hpg=hpg, groups=groups, nseq=nseq, rows=rows,
                          long_mode=long_mode),
        grid=grid, in_specs=[row_spec] * 6 + [s_spec], out_specs=[row_spec, s_spec],
        out_shape=[jax.ShapeDtypeStruct(shape4, F32), jax.ShapeDtypeStruct(s_in.shape, F32)],
        scratch_shapes=scratch, compiler_params=_params(sem), name="wkv7_chunked",
    )(*ins, s_in)
    return y.reshape(total, c), s_out


def _rwkv_out_kernel(y_ref, bonus_ref, sz_ref, x_ref, gng_ref, gnb_ref, wo_ref, ones_ref, o_ref):
    y = y_ref[...]
    inv_n = 1.0 / HEAD_DIM
    mean = _head_sum(y, ones_ref) * inv_n
    d = y - mean
    var = _head_sum(d * d, ones_ref) * inv_n
    yn = d * lax.rsqrt(var + GN_EPS) * gng_ref[...] + gnb_ref[...]
    gated = (yn + bonus_ref[...]) * sz_ref[...]
    o_ref[...] = x_ref[...] + _bdot(gated, wo_ref[...])


def _rwkv_out(y, bonus, sz, x2d, p):
    rows, c = y.shape
    d = x2d.shape[1]
    tile = min(OUT_ROW_TILE, rows)
    row_c = pl.BlockSpec((tile, c), lambda i: (i, 0))
    row_d = pl.BlockSpec((tile, d), lambda i: (i, 0))
    consts = [p['gn_g'], p['gn_b'], p['w_o'], p['ones']]
    return pl.pallas_call(
        _rwkv_out_kernel, grid=(rows // tile,),
        in_specs=[row_c, row_c, row_c, row_d] + [_const_spec(a.shape) for a in consts],
        out_specs=row_d, out_shape=jax.ShapeDtypeStruct((rows, d), F32),
        compiler_params=_params(("arbitrary",)), name="rwkv_out",
    )(y, bonus, sz, x2d, *consts)


def _pool_kernel(*refs, seq_len, tile, long_mode, pos0, final):
    it = iter(refs)
    x_ref, buf_ref, norm_ref, win_ref, wgrp_ref, bgrp_ref, scale_ref, wo_ref = (next(it) for _ in range(8))
    fnorm_ref = next(it) if final else None
    o_ref, nbuf_ref = next(it), next(it)
    u_ref, ext_ref, p_ref = next(it), next(it), next(it)
    carry_ref = next(it) if long_mode else None

    c = u_ref.shape[1]
    pg = c // len(POOL_WINDOWS)
    tl = tile if long_mode else seq_len
    x = x_ref[...]
    xn = _rms_norm(x, norm_ref[...])
    uz = _bdot(xn, win_ref[...])
    u_ref[...] = uz[:, :c]
    z = uz[:, c:]

    if long_mode:
        t = pl.program_id(1)
        row0 = t * tile
    else:
        row0 = 0
    pos = pos0 + row0 + lax.broadcasted_iota(jnp.int32, (tl, 1), 0)

    nbuf_rows = nbuf_ref.shape[1]

    def pool_one(base):
        ext_ref[POOL_HALO:POOL_HALO + tl, :] = u_ref[pl.ds(base, tl), :]
        for g, win in enumerate(POOL_WINDOWS):
            lanes = slice(g * pg, (g + 1) * pg)
            s = ext_ref[:, lanes]
            dlt = 1
            while dlt < win:
                s = s + pltpu.roll(s, dlt, 0)
                dlt *= 2
            cnt = jnp.minimum(win, pos + 1).astype(F32)
            cur = ext_ref[POOL_HALO:POOL_HALO + tl, lanes]
            p_ref[pl.ds(base, tl), lanes] = s[POOL_HALO:POOL_HALO + tl, :] / cnt - cur

    def load_halo(seq):
        ext_ref[0:POOL_HALO - nbuf_rows, :] = jnp.zeros((POOL_HALO - nbuf_rows, c), F32)
        ext_ref[POOL_HALO - nbuf_rows:POOL_HALO, :] = buf_ref[seq]

    if long_mode:
        @pl.when(t == 0)
        def _():
            load_halo(0)

        @pl.when(t > 0)
        def _():
            ext_ref[0:POOL_HALO, :] = carry_ref[...]

        pool_one(0)
        carry_ref[...] = ext_ref[tl:tl + POOL_HALO, :]
        nbuf_ref[0] = ext_ref[tl + POOL_HALO - nbuf_rows:tl + POOL_HALO, :]
    else:
        def body(s, carry):
            load_halo(s)
            pool_one(pl.multiple_of(s * seq_len, seq_len))
            nbuf_ref[s] = ext_ref[tl + POOL_HALO - nbuf_rows:tl + POOL_HALO, :]
            return carry

        lax.fori_loop(0, tile // seq_len, body, 0)

    p = p_ref[...]
    mixed = jnp.concatenate(
        [_bdot(p[:, g * pg:(g + 1) * pg], wgrp_ref[g]) for g in range(len(POOL_WINDOWS))], axis=1)
    mixed = (mixed + bgrp_ref[...]) * scale_ref[...]
    gated = mixed * (z * _sigmoid(z))
    out = x + _bdot(gated, wo_ref[...])
    if final:
        out = _rms_norm(out, fnorm_ref[...])
    o_ref[...] = out


def _pool_layer(x2d, buf, seq_len, pos0, p, final_norm):
    rows, d = x2d.shape
    c = p['w_o'].shape[0]
    nb = rows // seq_len
    long_mode = seq_len >= POOL_ROW_TILE
    final = final_norm is not None
    nbuf_rows = buf.shape[1]
    if long_mode:
        tile = POOL_ROW_TILE
        tps = seq_len // tile
        grid = (nb, tps)
        row_spec = pl.BlockSpec((tile, d), lambda b, t: (b * tps + t, 0))
        buf_spec = pl.BlockSpec((1, nbuf_rows, c), lambda b, t: (b, 0, 0))
        tl = tile
        sem = ("arbitrary", "arbitrary")
    else:
        tile = min(POOL_ROW_TILE, rows)
        grid = (rows // tile,)
        row_spec = pl.BlockSpec((tile, d), lambda i: (i, 0))
        buf_spec = pl.BlockSpec((tile // seq_len, nbuf_rows, c), lambda i: (i, 0, 0))
        tl = seq_len
        sem = ("arbitrary",)
    consts = [p['norm'], p['w_in'], p['w_grp'], p['b_grp'], p['scale'], p['w_o']]
    if final:
        consts.append(final_norm)
    scratch = [pltpu.VMEM((tile, c), F32), pltpu.VMEM((POOL_HALO + tl, c), F32), pltpu.VMEM((tile, c), F32)]
    if long_mode:
        scratch.append(pltpu.VMEM((POOL_HALO, c), F32))
    return pl.pallas_call(
        functools.partial(_pool_kernel, seq_len=seq_len, tile=tile, long_mode=long_mode, pos0=pos0,
                          final=final),
        grid=grid, in_specs=[row_spec, buf_spec] + [_const_spec(a.shape) for a in consts],
        out_specs=[row_spec, buf_spec],
        out_shape=[jax.ShapeDtypeStruct((rows, d), F32), jax.ShapeDtypeStruct((nb, nbuf_rows, c), F32)],
        scratch_shapes=scratch, compiler_params=_params(sem), name="pool_mixer",
    )(x2d, buf, *consts)


def _trunk(x, shift0, wkv0, buf0, pos0, rw, pw, vres, final_norm):
    nb, seq_len, d = x.shape
    x2d = x.reshape(nb * seq_len, d)
    new_shift, new_wkv, new_buf = [], [], []
    v_first = None
    depth = len(rw) + len(pw)
    for i in range(depth):
        j = i // 2
        if i % 2 == 0:
            p = rw[j]
            if seq_len >= ROW_TILE:
                halo = shift0[j][:, None, :]
            else:
                halo = jnp.repeat(shift0[j], seq_len, axis=0)
            r, lw, kh, v, kk, ba, bonus, sz, xn_last = _rwkv_proj(
                x2d, halo, seq_len, p, None if j == 0 else vres[j - 1], v_first)
            if j == 0:
                v_first = v
            y, s_new = _wkv(r, lw, kh, v, kk, ba, wkv0[j], seq_len)
            x2d = _rwkv_out(y, bonus, sz, x2d, p)
            new_shift.append(xn_last.reshape(nb, -1, d)[:, -1, :])
            new_wkv.append(s_new)
        else:
            p = pw[j]
            x2d, nbuf = _pool_layer(x2d, buf0[j], seq_len, pos0, p,
                                    final_norm if i == depth - 1 else None)
            new_buf.append(nbuf)
    return (x2d.reshape(nb, seq_len, d), jnp.stack(new_shift), jnp.stack(new_wkv), jnp.stack(new_buf))


def kernel(x_prompt, x_sample, state_shift, state_wkv, state_pool, rwkv_norm, rwkv_mu, rwkv_w_r, rwkv_w_k, rwkv_w_v, rwkv_w_z, rwkv_w0, rwkv_w1, rwkv_w2, rwkv_a0, rwkv_a1, rwkv_a2, rwkv_v0, rwkv_v1, rwkv_v2, rwkv_k_k, rwkv_k_a, rwkv_r_k, rwkv_gn_g, rwkv_gn_b, rwkv_w_o, pool_norm, pool_w_in, pool_w_grp, pool_b_grp, pool_scale, pool_w_o, final_norm):
    n_rwkv, d, c = rwkv_w_r.shape
    n_pool = pool_w_in.shape[0]
    n_heads = c // HEAD_DIM
    bf = lambda a: a.astype(BF16)
    row = lambda a: a.reshape(1, -1)
    slab = min(c, HEAD_SUM_SLAB)
    head_of_lane = jnp.arange(slab, dtype=jnp.int32) // HEAD_DIM
    ones = (head_of_lane[:, None] == head_of_lane[None, :]).astype(BF16)
    rw = [dict(norm=row(rwkv_norm[j]), mu=rwkv_mu[j], w_r=bf(rwkv_w_r[j]), w_k=bf(rwkv_w_k[j]),
               w_v=bf(rwkv_w_v[j]), w_z=bf(rwkv_w_z[j]), w0=row(rwkv_w0[j]), w1=bf(rwkv_w1[j]),
               w2=bf(rwkv_w2[j]), a0=row(rwkv_a0[j]), a1=bf(rwkv_a1[j]), a2=bf(rwkv_a2[j]),
               k_k=row(rwkv_k_k[j]), k_a=row(rwkv_k_a[j]), r_k=row(rwkv_r_k[j]),
               gn_g=row(rwkv_gn_g[j]), gn_b=row(rwkv_gn_b[j]), w_o=bf(rwkv_w_o[j]), ones=ones)
          for j in range(n_rwkv)]
    vres = [(row(rwkv_v0[j]), bf(rwkv_v1[j]), bf(rwkv_v2[j])) for j in range(rwkv_v0.shape[0])]
    pw = [dict(norm=row(pool_norm[j]), w_in=bf(pool_w_in[j]), w_grp=bf(pool_w_grp[j]),
               b_grp=row(pool_b_grp[j]), scale=row(pool_scale[j]), w_o=bf(pool_w_o[j]))
          for j in range(n_pool)]
    fn = row(final_norm)

    nb = x_prompt.shape[0]
    shift0 = jnp.zeros((n_rwkv, nb, d), state_shift.dtype)
    wkv0 = jnp.zeros((n_rwkv, nb, n_heads, HEAD_DIM, HEAD_DIM), state_wkv.dtype)
    buf0 = jnp.zeros((n_pool, nb, state_pool.shape[2], c), state_pool.dtype)
    y_p, sh_p, wkv_p, pool_p = _trunk(x_prompt, shift0, wkv0, buf0, 0, rw, pw, vres, fn)
    y_s, sh_s, wkv_s, pool_s = _trunk(x_sample, state_shift, state_wkv, state_pool, PAST_LEN, rw, pw,
                                      vres, fn)
    return (y_p, y_s, sh_p, wkv_p, pool_p, sh_s, wkv_s, pool_s)
```

```python
import functools
import math

import jax
import jax.numpy as jnp
from jax import lax
from jax.experimental import pallas as pl
from jax.experimental.pallas import tpu as pltpu

F32 = jnp.float32
BF16 = jnp.bfloat16

HEAD_DIM = 64
POOL_WINDOWS = (2, 4, 8, 16)
POOL_HALO = 16
NORM_EPS = 1e-6
GN_EPS = 64e-5
PAST_LEN = 16384

VMEM_LIMIT_BYTES = 56 * 1024 * 1024
ROW_TILE = 256
PROJ_SLAB = 256
OUT_ROW_TILE = 256
POOL_ROW_TILE = 256
HEAD_SUM_SLAB = 256
WKV_ROW_TILE = 64
WKV_SEQS_PER_STEP = 2
WKV_SHORT_SEQS_PER_STEP = 4
WKV_CHUNK = 32
WKV_HEADS_PER_GROUP = 4


def _bdot(a, b):
    return jnp.dot(a.astype(BF16), b.astype(BF16), preferred_element_type=F32)


def _bdot_nt(a, b):
    return lax.dot_general(a.astype(BF16), b.astype(BF16), (((1,), (1,)), ((), ())),
                           preferred_element_type=F32)


def _bdot_tn(a, b):
    return lax.dot_general(a.astype(BF16), b.astype(BF16), (((0,), (0,)), ((), ())),
                           preferred_element_type=F32)


def _split2(x):
    hi = x.astype(BF16)
    lo = (x - hi.astype(F32)).astype(BF16)
    return hi, lo


def _head_sum(x, ones_ref):
    sel = ones_ref[...]
    b = sel.shape[0]
    xb = x.astype(BF16)
    return jnp.concatenate([jnp.dot(xb[:, i:i + b], sel, preferred_element_type=F32)
                            for i in range(0, x.shape[1], b)], axis=1)


def _sum(xs):
    return functools.reduce(lambda a, b: a + b, xs)


def _rms_norm(x, g):
    return x * lax.rsqrt(jnp.mean(x * x, axis=-1, keepdims=True) + NORM_EPS) * g


def _sigmoid(x):
    return 1.0 / (1.0 + jnp.exp(-x))


def _const_spec(shape):
    nd = len(shape)
    return pl.BlockSpec(shape, lambda *_: (0,) * nd, pipeline_mode=pl.Buffered(1))


def _params(sem):
    return pltpu.CompilerParams(dimension_semantics=sem, vmem_limit_bytes=VMEM_LIMIT_BYTES)


def _rwkv_proj_kernel(*refs, seq_len, tile, long_mode, has_vres):
    it = iter(refs)
    x_ref = next(it)
    halo_ref = next(it)
    norm_ref, mu_ref, wr_ref, wk_ref, wv_ref, wz_ref = (next(it) for _ in range(6))
    w0_ref, w1_ref, w2_ref, a0_ref, a1_ref, a2_ref = (next(it) for _ in range(6))
    kk_ref, ka_ref, rk_ref, ones_ref = (next(it) for _ in range(4))
    if has_vres:
        v0_ref, v1_ref, v2_ref, vfirst_ref = (next(it) for _ in range(4))
    r_o, lw_o, kh_o, v_o, kkn_o, ba_o, bonus_o, sz_o, xn_o = (next(it) for _ in range(9))
    carry_ref = next(it) if long_mode else None

    x = x_ref[...]
    xn = _rms_norm(x, norm_ref[...])
    rolled = pltpu.roll(xn, 1, 0)
    row = lax.broadcasted_iota(jnp.int32, xn.shape, 0)
    if long_mode:
        t = pl.program_id(1)

        @pl.when(t == 0)
        def _():
            carry_ref[...] = jnp.broadcast_to(halo_ref[0], carry_ref.shape)

        x_prev = jnp.where(row == 0, carry_ref[7:8, :], rolled)
        carry_ref[...] = xn[tile - 8:tile, :]
        xn_o[0] = xn[tile - 1:tile, :]
    else:
        x_prev = jnp.where((row & (seq_len - 1)) == 0, halo_ref[...], rolled)
        xn_o[...] = xn
    dx = x_prev - xn
    mu = mu_ref[...]
    xr, xw, xk, xv, xa, xg = ((xn + dx * mu[m:m + 1, :]).astype(BF16) for m in range(6))
    lora_w = jnp.tanh(_bdot(xw, w1_ref[...]))
    lora_a = _bdot(xa, a1_ref[...])
    if has_vres:
        lora_v = _bdot(xv, v1_ref[...])

    c = wr_ref.shape[1]
    slab = min(c, PROJ_SLAB)
    for j in range(0, c, slab):
        sl = slice(j, j + slab)
        r = _bdot(xr, wr_ref[:, sl])
        k = _bdot(xk, wk_ref[:, sl])
        v = _bdot(xv, wv_ref[:, sl])
        z = _bdot(xg, wz_ref[:, sl])

        neg = -(w0_ref[:, sl] + _bdot(lora_w, w2_ref[:, sl]))
        softplus = jnp.maximum(neg, 0.0) + jnp.log(1.0 + jnp.exp(-jnp.abs(neg)))
        lw_o[:, sl] = -jnp.exp(-softplus - 0.5)

        a = _sigmoid(a0_ref[:, sl] + _bdot(lora_a, a2_ref[:, sl]))
        if has_vres:
            gate = _sigmoid(v0_ref[:, sl] + _bdot(lora_v, v2_ref[:, sl]))
            v = v + (vfirst_ref[:, sl].astype(F32) - v) * gate

        kk = k * kk_ref[:, sl]
        nrm = jnp.maximum(jnp.sqrt(_head_sum(kk * kk, ones_ref)), 1e-12)
        kk = kk / nrm
        kh = k * (1.0 + (a - 1.0) * ka_ref[:, sl])
        bonus_o[:, sl] = _head_sum(r * kh * rk_ref[:, sl], ones_ref) * v
        r_o[:, sl] = r.astype(r_o.dtype)
        kh_o[:, sl] = kh.astype(kh_o.dtype)
        v_o[:, sl] = v.astype(v_o.dtype)
        kkn_o[:, sl] = kk.astype(kkn_o.dtype)
        ba_o[:, sl] = (kk * a).astype(ba_o.dtype)
        sz_o[:, sl] = z * _sigmoid(z)


def _rwkv_proj(x2d, halo, seq_len, p, vres, vfirst):
    rows, d = x2d.shape
    c = p['w_r'].shape[1]
    nb = rows // seq_len
    long_mode = seq_len >= ROW_TILE
    tile = ROW_TILE if long_mode else min(ROW_TILE, rows)
    has_vres = vres is not None
    if long_mode:
        tps = seq_len // tile
        grid = (nb, tps)
        row_map = lambda b, t: (b * tps + t, 0)
        halo_spec = pl.BlockSpec((1, 1, d), lambda b, t: (b, 0, 0))
        xn_shape = jax.ShapeDtypeStruct((nb, 1, d), F32)
        xn_spec = pl.BlockSpec((1, 1, d), lambda b, t: (b, 0, 0))
        scratch = [pltpu.VMEM((8, d), F32)]
        sem = ("arbitrary", "arbitrary")
    else:
        grid = (rows // tile,)
        row_map = lambda i: (i, 0)
        halo_spec = pl.BlockSpec((tile, d), row_map)
        xn_shape = jax.ShapeDtypeStruct((rows, d), F32)
        xn_spec = pl.BlockSpec((tile, d), row_map)
        scratch = []
        sem = ("arbitrary",)
    row_d = pl.BlockSpec((tile, d), row_map)
    row_c = pl.BlockSpec((tile, c), row_map)
    consts = [p['norm'], p['mu'], p['w_r'], p['w_k'], p['w_v'], p['w_z'], p['w0'], p['w1'], p['w2'],
              p['a0'], p['a1'], p['a2'], p['k_k'], p['k_a'], p['r_k'], p['ones']]
    args = [x2d, halo] + consts
    in_specs = [row_d, halo_spec] + [_const_spec(a.shape) for a in consts]
    if has_vres:
        args += list(vres) + [vfirst]
        in_specs += [_const_spec(a.shape) for a in vres] + [row_c]
    out_dtypes = [BF16, F32, BF16, BF16, BF16, BF16, F32, F32]
    out_shape = [jax.ShapeDtypeStruct((rows, c), dt) for dt in out_dtypes] + [xn_shape]
    out_specs = [row_c] * 8 + [xn_spec]
    return pl.pallas_call(
        functools.partial(_rwkv_proj_kernel, seq_len=seq_len, tile=tile, long_mode=long_mode,
                          has_vres=has_vres),
        grid=grid, in_specs=in_specs, out_specs=out_specs, out_shape=out_shape,
        scratch_shapes=scratch, compiler_params=_params(sem), name="rwkv_proj",
    )(*args)


def _wkv_masks(chunk, hpg, op_dtype):
    w = HEAD_DIM * hpg
    n = chunk * hpg
    sh = int(math.log2(chunk))
    lane = lax.broadcasted_iota(jnp.int32, (1, w), 1)
    op_masks = [((lane >> 6) == e).astype(op_dtype) for e in range(hpg)]
    t2 = lax.broadcasted_iota(jnp.int32, (chunk, n), 0)
    s2 = lax.broadcasted_iota(jnp.int32, (chunk, n), 1) & (chunk - 1)
    strict = (s2 < t2).astype(F32)
    incl = (s2 <= t2).astype(F32)
    eye_row = (s2 == t2).astype(F32)
    rb = lax.broadcasted_iota(jnp.int32, (n, n), 0)
    cb = lax.broadcasted_iota(jnp.int32, (n, n), 1)
    bmask = ((rb >> sh) == (cb >> sh)).astype(F32)
    return op_masks, strict, incl, eye_row, bmask


def _load_state(s_ref, seq, gi, hpg):
    return jnp.concatenate([s_ref[seq, gi * hpg + e] for e in range(hpg)], axis=1)


def _store_state(s_ref, seq, gi, hpg, s_cat):
    for e in range(hpg):
        s_ref[seq, gi * hpg + e] = s_cat[:, e * HEAD_DIM:(e + 1) * HEAD_DIM]


def _wkv_prepass(r_ref, lw_ref, k_ref, v_ref, kk_ref, ba_ref, ops, clast_ref, chunk):
    at_ref, rt_ref, bt_ref, kt_ref, bh_ref, kh_ref, vb_ref = ops
    nr, c_dim = at_ref.shape
    flat = lambda ref: ref[...].astype(F32).reshape(nr, c_dim)
    sh = int(math.log2(chunk))
    ti = lax.broadcasted_iota(jnp.int32, (nr, nr), 0)
    si = lax.broadcasted_iota(jnp.int32, (nr, nr), 1)
    tri = (((ti >> sh) == (si >> sh)) & (ti >= si)).astype(BF16)
    lw = flat(lw_ref)
    c = _sum([jnp.dot(tri, q, preferred_element_type=F32) for q in _split2(lw)])
    c_last = jnp.concatenate(
        [jnp.broadcast_to(c[q + chunk - 1:q + chunk, :], (chunk, c_dim)) for q in range(0, nr, chunk)], axis=0)
    dt = at_ref.dtype
    kk = flat(kk_ref)
    ba = flat(ba_ref)
    k = flat(k_ref)
    at_ref[...] = (-kk * jnp.exp(c - lw)).astype(dt)
    rt_ref[...] = (flat(r_ref) * jnp.exp(c)).astype(dt)
    e_inv = jnp.exp(-c)
    bt_ref[...] = (ba * e_inv).astype(dt)
    kt_ref[...] = (k * e_inv).astype(dt)
    e_hat = jnp.exp(c_last - c)
    bh_ref[...] = (ba * e_hat).astype(dt)
    kh_ref[...] = (k * e_hat).astype(dt)
    vb_ref[...] = flat(v_ref).astype(dt)
    clast_ref[...] = c_last


def _wkv_chunk_step(states, row_starts, ops, clast_ref, masks, chunk, hpg, groups):
    at_ref, rt_ref, bt_ref, kt_ref, bh_ref, kh_ref, vb_ref = ops
    op_masks, strict, incl, eye_row, bmask = masks
    low2 = jnp.concatenate([strict, incl], axis=0)
    w = HEAD_DIM * hpg
    chains = [(s, g) for s in range(len(row_starts)) for g in range(groups)]

    def tile(ref, s, g):
        return ref[pl.ds(row_starts[s], chunk), g * w:(g + 1) * w]

    def stack(m):
        return jnp.concatenate([m.astype(lm.dtype) * lm for lm in op_masks], axis=0)

    def block_diag(m):
        return jnp.concatenate([m] * hpg, axis=0) * bmask

    lhs2 = [jnp.concatenate([tile(at_ref, s, g), tile(rt_ref, s, g)], axis=0) for s, g in chains]
    v_t = [tile(vb_ref, s, g) for s, g in chains]
    v_st = [stack(x) for x in v_t]
    n = hpg * chunk
    sc = [_bdot_nt(l, jnp.concatenate([stack(tile(bt_ref, s, g)), stack(tile(kt_ref, s, g))], axis=0))
          for l, (s, g) in zip(lhs2, chains)]
    sr = [_bdot_nt(l, stack(states[s][g])) for l, (s, g) in zip(lhs2, chains)]

    x_ab = [m[:chunk, :n] * strict for m in sc]
    x_rb = [m[chunk:, :n] * incl for m in sc]

    pw = [_bdot(a, block_diag(a)) for a in x_ab]
    srk = [m + _bdot(q[:, n:] * low2, vs) for m, q, vs in zip(sr, sc, v_st)]
    xa = [m[:chunk] for m in srk]
    y_k = [m[chunk:] for m in srk]
    t_inv = [eye_row + a for a in x_ab]
    for _ in range(int(math.log2(chunk)) - 2):
        t_new = [t + _bdot(p, block_diag(t)) for t, p in zip(t_inv, pw)]
        pw = [_bdot(p, block_diag(p)) for p in pw]
        t_inv = t_new
    t_inv = [t + _bdot(p, block_diag(t)) for t, p in zip(t_inv, pw)]

    u = [_bdot(t, stack(x)) for t, x in zip(t_inv, xa)]
    y = [yk + _bdot(x, stack(uu)) for yk, x, uu in zip(y_k, x_rb, u)]
    upd = []
    for uu, vv, (s, g) in zip(u, v_t, chains):
        uv = jnp.concatenate([uu, vv.astype(F32)], axis=0)
        uv_heads = jnp.concatenate([uv[:, e * HEAD_DIM:(e + 1) * HEAD_DIM] for e in range(hpg)], axis=0)
        bk = stack(jnp.concatenate([tile(bh_ref, s, g), tile(kh_ref, s, g)], axis=0))
        upd.append(_bdot_tn(uv_heads, bk))

    new_states = [[None] * groups for _ in row_starts]
    ys = [[None] * groups for _ in row_starts]
    for (s, g), up, yy in zip(chains, upd, y):
        decay = jnp.exp(clast_ref[pl.ds(row_starts[s], 1), g * w:(g + 1) * w])
        new_states[s][g] = states[s][g] * decay + up
        ys[s][g] = yy
    return new_states, ys


def _wkv_kernel(r_ref, lw_ref, k_ref, v_ref, kk_ref, ba_ref, s_in_ref, y_ref, s_out_ref, *scratch,
                chunk, hpg, groups, nseq, rows, long_mode):
    ops = scratch[:7]
    clast_ref = scratch[7]
    w = HEAD_DIM * hpg
    _wkv_prepass(r_ref, lw_ref, k_ref, v_ref, kk_ref, ba_ref, ops, clast_ref, chunk)
    masks = _wkv_masks(chunk, hpg, ops[0].dtype)

    if long_mode:
        state_ref = scratch[8]
        t = pl.program_id(1)

        @pl.when(t == 0)
        def _():
            for s in range(nseq):
                for g in range(groups):
                    state_ref[s, g] = _load_state(s_in_ref, s, g, hpg)

        def body(ci, carry):
            starts = [pl.multiple_of(s * rows + ci * chunk, chunk) for s in range(nseq)]
            states = [[state_ref[s, g] for g in range(groups)] for s in range(nseq)]
            new_states, ys = _wkv_chunk_step(states, starts, ops, clast_ref, masks, chunk, hpg, groups)
            for s in range(nseq):
                for g in range(groups):
                    state_ref[s, g] = new_states[s][g]
                    y_ref[s, pl.ds(pl.multiple_of(ci * chunk, chunk), chunk), g * w:(g + 1) * w] = ys[s][g]
            return carry

        lax.fori_loop(0, rows // chunk, body, 0)

        @pl.when(t == pl.num_programs(1) - 1)
        def _():
            for s in range(nseq):
                for g in range(groups):
                    _store_state(s_out_ref, s, g, hpg, state_ref[s, g])
    else:
        starts = [s * rows for s in range(nseq)]
        states = [[_load_state(s_in_ref, s, g, hpg) for g in range(groups)] for s in range(nseq)]
        new_states, ys = _wkv_chunk_step(states, starts, ops, clast_ref, masks, chunk, hpg, groups)
        for s in range(nseq):
            for g in range(groups):
                _store_state(s_out_ref, s, g, hpg, new_states[s][g])
                y_ref[s, :, g * w:(g + 1) * w] = ys[s][g]


def _wkv(r, lw, kh, v, kk, ba, s_all, layer, seq_len):
    total, c = r.shape
    nb = total // seq_len
    hpg = WKV_HEADS_PER_GROUP
    groups = c // (HEAD_DIM * hpg)
    n_heads = c // HEAD_DIM
    long_mode = seq_len > WKV_CHUNK
    if long_mode:
        chunk, nseq, rows = WKV_CHUNK, WKV_SEQS_PER_STEP, min(WKV_ROW_TILE, seq_len)
        tps = seq_len // rows
        grid = (nb // nseq, tps)
        row_spec = pl.BlockSpec((nseq, None, rows, c), lambda b, t: (b, t, 0, 0))
        s_spec = pl.BlockSpec((None, nseq, n_heads, HEAD_DIM, HEAD_DIM), lambda b, t: (layer, b, 0, 0, 0))
        shape4 = (nb, tps, rows, c)
        op_dtype = BF16
        sem = ("arbitrary", "arbitrary")
    else:
        chunk, nseq, rows = seq_len, WKV_SHORT_SEQS_PER_STEP, seq_len
        grid = (nb // nseq,)
        row_spec = pl.BlockSpec((nseq, None, rows, c), lambda b: (b, 0, 0, 0))
        s_spec = pl.BlockSpec((None, nseq, n_heads, HEAD_DIM, HEAD_DIM), lambda b: (layer, b, 0, 0, 0))
        shape4 = (nb, 1, rows, c)
        op_dtype = F32
        sem = ("arbitrary",)
    scratch = [pltpu.VMEM((nseq * rows, c), op_dtype)] * 7 + [pltpu.VMEM((nseq * rows, c), F32)]
    if long_mode:
        scratch.append(pltpu.VMEM((nseq, groups, HEAD_DIM, HEAD_DIM * hpg), F32))
    ins = [a.reshape(shape4) for a in (r, lw, kh, v, kk, ba)]
    y, s_out = pl.pallas_call(
        functools.partial(_wkv_kernel, chunk=chunk, hpg=hpg, groups=groups, nseq=nseq, rows=rows,
                          long_mode=long_mode),
        grid=grid, in_specs=[row_spec] * 6 + [s_spec], out_specs=[row_spec, s_spec],
        out_shape=[jax.ShapeDtypeStruct(shape4, F32), jax.ShapeDtypeStruct(s_all.shape, F32)],
        input_output_aliases={len(ins): 1},
        scratch_shapes=scratch, compiler_params=_params(sem), name="wkv7_chunked",
    )(*ins, s_all)
    return y.reshape(total, c), s_out


def _rwkv_out_kernel(y_ref, bonus_ref, sz_ref, x_ref, gng_ref, gnb_ref, wo_ref, ones_ref, o_ref):
    y = y_ref[...]
    inv_n = 1.0 / HEAD_DIM
    mean = _head_sum(y, ones_ref) * inv_n
    d = y - mean
    var = _head_sum(d * d, ones_ref) * inv_n
    yn = d * lax.rsqrt(var + GN_EPS) * gng_ref[...] + gnb_ref[...]
    gated = (yn + bonus_ref[...]) * sz_ref[...]
    o_ref[...] = x_ref[...] + _bdot(gated, wo_ref[...])


def _rwkv_out(y, bonus, sz, x2d, p):
    rows, c = y.shape
    d = x2d.shape[1]
    tile = min(OUT_ROW_TILE, rows)
    row_c = pl.BlockSpec((tile, c), lambda i: (i, 0))
    row_d = pl.BlockSpec((tile, d), lambda i: (i, 0))
    consts = [p['gn_g'], p['gn_b'], p['w_o'], p['ones']]
    return pl.pallas_call(
        _rwkv_out_kernel, grid=(rows // tile,),
        in_specs=[row_c, row_c, row_c, row_d] + [_const_spec(a.shape) for a in consts],
        out_specs=row_d, out_shape=jax.ShapeDtypeStruct((rows, d), F32),
        compiler_params=_params(("arbitrary",)), name="rwkv_out",
    )(y, bonus, sz, x2d, *consts)


def _pool_kernel(*refs, seq_len, tile, long_mode, pos0, final):
    it = iter(refs)
    x_ref, buf_ref, norm_ref, win_ref, wgrp_ref, bgrp_ref, scale_ref, wo_ref = (next(it) for _ in range(8))
    fnorm_ref = next(it) if final else None
    o_ref, nbuf_ref = next(it), next(it)
    u_ref, ext_ref, p_ref = next(it), next(it), next(it)
    carry_ref = next(it) if long_mode else None

    c = u_ref.shape[1]
    pg = c // len(POOL_WINDOWS)
    tl = tile if long_mode else seq_len
    x = x_ref[...]
    xn = _rms_norm(x, norm_ref[...])
    uz = _bdot(xn, win_ref[...])
    u_ref[...] = uz[:, :c]
    z = uz[:, c:]

    if long_mode:
        t = pl.program_id(1)
        row0 = t * tile
    else:
        row0 = 0
    pos = pos0 + row0 + lax.broadcasted_iota(jnp.int32, (tl, 1), 0)

    nbuf_rows = nbuf_ref.shape[1]

    def pool_one(base):
        ext_ref[POOL_HALO:POOL_HALO + tl, :] = u_ref[pl.ds(base, tl), :]
        for g, win in enumerate(POOL_WINDOWS):
            lanes = slice(g * pg, (g + 1) * pg)
            s = ext_ref[:, lanes]
            dlt = 1
            while dlt < win:
                s = s + pltpu.roll(s, dlt, 0)
                dlt *= 2
            cnt = jnp.minimum(win, pos + 1).astype(F32)
            cur = ext_ref[POOL_HALO:POOL_HALO + tl, lanes]
            p_ref[pl.ds(base, tl), lanes] = s[POOL_HALO:POOL_HALO + tl, :] / cnt - cur

    def load_halo(seq):
        ext_ref[0:POOL_HALO - nbuf_rows, :] = jnp.zeros((POOL_HALO - nbuf_rows, c), F32)
        ext_ref[POOL_HALO - nbuf_rows:POOL_HALO, :] = buf_ref[seq]

    if long_mode:
        @pl.when(t == 0)
        def _():
            load_halo(0)

        @pl.when(t > 0)
        def _():
            ext_ref[0:POOL_HALO, :] = carry_ref[...]

        pool_one(0)
        carry_ref[...] = ext_ref[tl:tl + POOL_HALO, :]
        nbuf_ref[0] = ext_ref[tl + POOL_HALO - nbuf_rows:tl + POOL_HALO, :]
    else:
        def body(s, carry):
            load_halo(s)
            pool_one(pl.multiple_of(s * seq_len, seq_len))
            nbuf_ref[s] = ext_ref[tl + POOL_HALO - nbuf_rows:tl + POOL_HALO, :]
            return carry

        lax.fori_loop(0, tile // seq_len, body, 0)

    p = p_ref[...]
    mixed = jnp.concatenate(
        [_bdot(p[:, g * pg:(g + 1) * pg], wgrp_ref[g]) for g in range(len(POOL_WINDOWS))], axis=1)
    mixed = (mixed + bgrp_ref[...]) * scale_ref[...]
    gated = mixed * (z * _sigmoid(z))
    out = x + _bdot(gated, wo_ref[...])
    if final:
        out = _rms_norm(out, fnorm_ref[...])
    o_ref[...] = out


def _pool_layer(x2d, buf_all, layer, seq_len, pos0, p, final_norm):
    rows, d = x2d.shape
    c = p['w_o'].shape[0]
    nb = rows // seq_len
    long_mode = seq_len >= POOL_ROW_TILE
    final = final_norm is not None
    nbuf_rows = buf_all.shape[2]
    if long_mode:
        tile = POOL_ROW_TILE
        tps = seq_len // tile
        grid = (nb, tps)
        row_spec = pl.BlockSpec((tile, d), lambda b, t: (b * tps + t, 0))
        buf_spec = pl.BlockSpec((None, 1, nbuf_rows, c), lambda b, t: (layer, b, 0, 0))
        tl = tile
        sem = ("arbitrary", "arbitrary")
    else:
        tile = min(POOL_ROW_TILE, rows)
        grid = (rows // tile,)
        row_spec = pl.BlockSpec((tile, d), lambda i: (i, 0))
        buf_spec = pl.BlockSpec((None, tile // seq_len, nbuf_rows, c), lambda i: (layer, i, 0, 0))
        tl = seq_len
        sem = ("arbitrary",)
    consts = [p['norm'], p['w_in'], p['w_grp'], p['b_grp'], p['scale'], p['w_o']]
    if final:
        consts.append(final_norm)
    scratch = [pltpu.VMEM((tile, c), F32), pltpu.VMEM((POOL_HALO + tl, c), F32), pltpu.VMEM((tile, c), F32)]
    if long_mode:
        scratch.append(pltpu.VMEM((POOL_HALO, c), F32))
    return pl.pallas_call(
        functools.partial(_pool_kernel, seq_len=seq_len, tile=tile, long_mode=long_mode, pos0=pos0,
                          final=final),
        grid=grid, in_specs=[row_spec, buf_spec] + [_const_spec(a.shape) for a in consts],
        out_specs=[row_spec, buf_spec],
        out_shape=[jax.ShapeDtypeStruct((rows, d), F32), jax.ShapeDtypeStruct(buf_all.shape, F32)],
        input_output_aliases={1: 1},
        scratch_shapes=scratch, compiler_params=_params(sem), name="pool_mixer",
    )(x2d, buf_all, *consts)


def _trunk(x, shift0, wkv0, buf0, pos0, rw, pw, vres, final_norm):
    nb, seq_len, d = x.shape
    x2d = x.reshape(nb * seq_len, d)
    new_shift = []
    v_first = None
    depth = len(rw) + len(pw)
    for i in range(depth):
        j = i // 2
        if i % 2 == 0:
            p = rw[j]
            if seq_len >= ROW_TILE:
                halo = shift0[j][:, None, :]
            else:
                halo = jnp.repeat(shift0[j], seq_len, axis=0)
            r, lw, kh, v, kk, ba, bonus, sz, xn_last = _rwkv_proj(
                x2d, halo, seq_len, p, None if j == 0 else vres[j - 1], v_first)
            if j == 0:
                v_first = v
            y, wkv0 = _wkv(r, lw, kh, v, kk, ba, wkv0, j, seq_len)
            x2d = _rwkv_out(y, bonus, sz, x2d, p)
            new_shift.append(xn_last.reshape(nb, -1, d)[:, -1, :])
        else:
            p = pw[j]
            x2d, buf0 = _pool_layer(x2d, buf0, j, seq_len, pos0, p,
                                    final_norm if i == depth - 1 else None)
    return x2d.reshape(nb, seq_len, d), jnp.stack(new_shift), wkv0, buf0


def kernel(x_prompt, x_sample, state_shift, state_wkv, state_pool, rwkv_norm, rwkv_mu, rwkv_w_r, rwkv_w_k, rwkv_w_v, rwkv_w_z, rwkv_w0, rwkv_w1, rwkv_w2, rwkv_a0, rwkv_a1, rwkv_a2, rwkv_v0, rwkv_v1, rwkv_v2, rwkv_k_k, rwkv_k_a, rwkv_r_k, rwkv_gn_g, rwkv_gn_b, rwkv_w_o, pool_norm, pool_w_in, pool_w_grp, pool_b_grp, pool_scale, pool_w_o, final_norm):
    n_rwkv, d, c = rwkv_w_r.shape
    n_pool = pool_w_in.shape[0]
    n_heads = c // HEAD_DIM
    bf = lambda a: a.astype(BF16)
    row = lambda a: a.reshape(1, -1)
    slab = min(c, HEAD_SUM_SLAB)
    head_of_lane = jnp.arange(slab, dtype=jnp.int32) // HEAD_DIM
    ones = (head_of_lane[:, None] == head_of_lane[None, :]).astype(BF16)
    rw = [dict(norm=row(rwkv_norm[j]), mu=rwkv_mu[j], w_r=bf(rwkv_w_r[j]), w_k=bf(rwkv_w_k[j]),
               w_v=bf(rwkv_w_v[j]), w_z=bf(rwkv_w_z[j]), w0=row(rwkv_w0[j]), w1=bf(rwkv_w1[j]),
               w2=bf(rwkv_w2[j]), a0=row(rwkv_a0[j]), a1=bf(rwkv_a1[j]), a2=bf(rwkv_a2[j]),
               k_k=row(rwkv_k_k[j]), k_a=row(rwkv_k_a[j]), r_k=row(rwkv_r_k[j]),
               gn_g=row(rwkv_gn_g[j]), gn_b=row(rwkv_gn_b[j]), w_o=bf(rwkv_w_o[j]), ones=ones)
          for j in range(n_rwkv)]
    vres = [(row(rwkv_v0[j]), bf(rwkv_v1[j]), bf(rwkv_v2[j])) for j in range(rwkv_v0.shape[0])]
    pw = [dict(norm=row(pool_norm[j]), w_in=bf(pool_w_in[j]), w_grp=bf(pool_w_grp[j]),
               b_grp=row(pool_b_grp[j]), scale=row(pool_scale[j]), w_o=bf(pool_w_o[j]))
          for j in range(n_pool)]
    fn = row(final_norm)

    nb = x_prompt.shape[0]
    shift0 = jnp.zeros((n_rwkv, nb, d), state_shift.dtype)
    wkv0 = jnp.zeros((n_rwkv, nb, n_heads, HEAD_DIM, HEAD_DIM), state_wkv.dtype)
    buf0 = jnp.zeros((n_pool, nb, state_pool.shape[2], c), state_pool.dtype)
    y_p, sh_p, wkv_p, pool_p = _trunk(x_prompt, shift0, wkv0, buf0, 0, rw, pw, vres, fn)
    y_s, sh_s, wkv_s, pool_s = _trunk(x_sample, state_shift, state_wkv, state_pool, PAST_LEN, rw, pw,
                                      vres, fn)
    return (y_p, y_s, sh_p, wkv_p, pool_p, sh_s, wkv_s, pool_s)
```

```python
import functools
import math

import jax
import jax.numpy as jnp
from jax import lax
from jax.experimental import pallas as pl
from jax.experimental.pallas import tpu as pltpu

F32 = jnp.float32
BF16 = jnp.bfloat16

HEAD_DIM = 64
POOL_WINDOWS = (2, 4, 8, 16)
POOL_HALO = 16
NORM_EPS = 1e-6
GN_EPS = 64e-5
PAST_LEN = 16384

VMEM_LIMIT_BYTES = 56 * 1024 * 1024
ROW_TILE = 256
PROJ_SLAB = 256
OUT_ROW_TILE = 256
POOL_ROW_TILE = 256
HEAD_SUM_SLAB = 256
WKV_ROW_TILE = 64
WKV_SEQS_PER_STEP = 2
WKV_SHORT_SEQS_PER_STEP = 4
WKV_CHUNK = 32
WKV_HEADS_PER_GROUP = 4


def _bdot(a, b):
    return jnp.dot(a.astype(BF16), b.astype(BF16), preferred_element_type=F32)


def _bdot_nt(a, b):
    return lax.dot_general(a.astype(BF16), b.astype(BF16), (((1,), (1,)), ((), ())),
                           preferred_element_type=F32)


def _bdot_tn(a, b):
    return lax.dot_general(a.astype(BF16), b.astype(BF16), (((0,), (0,)), ((), ())),
                           preferred_element_type=F32)


def _split2(x):
    hi = x.astype(BF16)
    lo = (x - hi.astype(F32)).astype(BF16)
    return hi, lo


def _head_sum(x, ones_ref):
    sel = ones_ref[...]
    b = sel.shape[0]
    xb = x.astype(BF16)
    return jnp.concatenate([jnp.dot(xb[:, i:i + b], sel, preferred_element_type=F32)
                            for i in range(0, x.shape[1], b)], axis=1)


def _sum(xs):
    return functools.reduce(lambda a, b: a + b, xs)


def _rms_norm(x, g):
    return x * lax.rsqrt(jnp.mean(x * x, axis=-1, keepdims=True) + NORM_EPS) * g


def _sigmoid(x):
    return 1.0 / (1.0 + jnp.exp(-x))


def _const_spec(shape):
    nd = len(shape)
    return pl.BlockSpec(shape, lambda *_: (0,) * nd, pipeline_mode=pl.Buffered(1))


def _params(sem):
    return pltpu.CompilerParams(dimension_semantics=sem, vmem_limit_bytes=VMEM_LIMIT_BYTES)


def _rwkv_proj_kernel(*refs, seq_len, tile, long_mode, has_vres):
    it = iter(refs)
    x_ref = next(it)
    halo_ref = next(it)
    norm_ref, mu_ref, wr_ref, wk_ref, wv_ref, wz_ref = (next(it) for _ in range(6))
    w0_ref, w1_ref, w2_ref, a0_ref, a1_ref, a2_ref = (next(it) for _ in range(6))
    kk_ref, ka_ref, rk_ref, ones_ref = (next(it) for _ in range(4))
    if has_vres:
        v0_ref, v1_ref, v2_ref, vfirst_ref = (next(it) for _ in range(4))
    r_o, lw_o, kh_o, v_o, kkn_o, ba_o, bonus_o, sz_o, xn_o = (next(it) for _ in range(9))
    carry_ref = next(it) if long_mode else None

    x = x_ref[...]
    xn = _rms_norm(x, norm_ref[...])
    rolled = pltpu.roll(xn, 1, 0)
    row = lax.broadcasted_iota(jnp.int32, xn.shape, 0)
    if long_mode:
        t = pl.program_id(1)

        @pl.when(t == 0)
        def _():
            carry_ref[...] = jnp.broadcast_to(halo_ref[0], carry_ref.shape)

        x_prev = jnp.where(row == 0, carry_ref[7:8, :], rolled)
        carry_ref[...] = xn[tile - 8:tile, :]
        xn_o[0] = xn[tile - 1:tile, :]
    else:
        x_prev = jnp.where((row & (seq_len - 1)) == 0, halo_ref[...], rolled)
        xn_o[...] = xn
    dx = x_prev - xn
    mu = mu_ref[...]
    xr, xw, xk, xv, xa, xg = ((xn + dx * mu[m:m + 1, :]).astype(BF16) for m in range(6))
    lora_w = jnp.tanh(_bdot(xw, w1_ref[...]))
    lora_a = _bdot(xa, a1_ref[...])
    if has_vres:
        lora_v = _bdot(xv, v1_ref[...])

    c = wr_ref.shape[1]
    slab = min(c, PROJ_SLAB)

    def slab_matmuls(j):
        sl = slice(j, j + slab)
        out = [_bdot(xr, wr_ref[:, sl]), _bdot(xk, wk_ref[:, sl]), _bdot(xv, wv_ref[:, sl]),
               _bdot(xg, wz_ref[:, sl]), _bdot(lora_w, w2_ref[:, sl]), _bdot(lora_a, a2_ref[:, sl])]
        if has_vres:
            out.append(_bdot(lora_v, v2_ref[:, sl]))
        return out

    starts = list(range(0, c, slab))
    ahead = slab_matmuls(starts[0])
    for idx, j in enumerate(starts):
        sl = slice(j, j + slab)
        r, k, v, z, w_lora, a_lora = ahead[:6]
        v_lora = ahead[6] if has_vres else None
        if idx + 1 < len(starts):
            ahead = slab_matmuls(starts[idx + 1])

        neg = -(w0_ref[:, sl] + w_lora)
        softplus = jnp.maximum(neg, 0.0) + jnp.log(1.0 + jnp.exp(-jnp.abs(neg)))
        lw_o[:, sl] = -jnp.exp(-softplus - 0.5)

        a = _sigmoid(a0_ref[:, sl] + a_lora)
        if has_vres:
            gate = _sigmoid(v0_ref[:, sl] + v_lora)
            v = v + (vfirst_ref[:, sl].astype(F32) - v) * gate

        kk = k * kk_ref[:, sl]
        nrm = jnp.maximum(jnp.sqrt(_head_sum(kk * kk, ones_ref)), 1e-12)
        kk = kk / nrm
        kh = k * (1.0 + (a - 1.0) * ka_ref[:, sl])
        bonus_o[:, sl] = (_head_sum(r * kh * rk_ref[:, sl], ones_ref) * v).astype(bonus_o.dtype)
        r_o[:, sl] = r.astype(r_o.dtype)
        kh_o[:, sl] = kh.astype(kh_o.dtype)
        v_o[:, sl] = v.astype(v_o.dtype)
        kkn_o[:, sl] = kk.astype(kkn_o.dtype)
        ba_o[:, sl] = (kk * a).astype(ba_o.dtype)
        sz_o[:, sl] = (z * _sigmoid(z)).astype(sz_o.dtype)


def _rwkv_proj(x2d, halo, seq_len, p, vres, vfirst):
    rows, d = x2d.shape
    c = p['w_r'].shape[1]
    nb = rows // seq_len
    long_mode = seq_len >= ROW_TILE
    tile = ROW_TILE if long_mode else min(ROW_TILE, rows)
    has_vres = vres is not None
    if long_mode:
        tps = seq_len // tile
        grid = (nb, tps)
        row_map = lambda b, t: (b * tps + t, 0)
        halo_spec = pl.BlockSpec((1, 1, d), lambda b, t: (b, 0, 0))
        xn_shape = jax.ShapeDtypeStruct((nb, 1, d), F32)
        xn_spec = pl.BlockSpec((1, 1, d), lambda b, t: (b, 0, 0))
        scratch = [pltpu.VMEM((8, d), F32)]
        sem = ("arbitrary", "arbitrary")
    else:
        grid = (rows // tile,)
        row_map = lambda i: (i, 0)
        halo_spec = pl.BlockSpec((tile, d), row_map)
        xn_shape = jax.ShapeDtypeStruct((rows, d), F32)
        xn_spec = pl.BlockSpec((tile, d), row_map)
        scratch = []
        sem = ("arbitrary",)
    row_d = pl.BlockSpec((tile, d), row_map)
    row_c = pl.BlockSpec((tile, c), row_map)
    consts = [p['norm'], p['mu'], p['w_r'], p['w_k'], p['w_v'], p['w_z'], p['w0'], p['w1'], p['w2'],
              p['a0'], p['a1'], p['a2'], p['k_k'], p['k_a'], p['r_k'], p['ones']]
    args = [x2d, halo] + consts
    in_specs = [row_d, halo_spec] + [_const_spec(a.shape) for a in consts]
    if has_vres:
        args += list(vres) + [vfirst]
        in_specs += [_const_spec(a.shape) for a in vres] + [row_c]
    out_dtypes = [BF16, F32, BF16, BF16, BF16, BF16, BF16, BF16]
    out_shape = [jax.ShapeDtypeStruct((rows, c), dt) for dt in out_dtypes] + [xn_shape]
    out_specs = [row_c] * 8 + [xn_spec]
    return pl.pallas_call(
        functools.partial(_rwkv_proj_kernel, seq_len=seq_len, tile=tile, long_mode=long_mode,
                          has_vres=has_vres),
        grid=grid, in_specs=in_specs, out_specs=out_specs, out_shape=out_shape,
        scratch_shapes=scratch, compiler_params=_params(sem), name="rwkv_proj",
    )(*args)


def _wkv_masks(chunk, hpg, op_dtype):
    w = HEAD_DIM * hpg
    n = chunk * hpg
    sh = int(math.log2(chunk))
    lane = lax.broadcasted_iota(jnp.int32, (1, w), 1)
    op_masks = [((lane >> 6) == e).astype(op_dtype) for e in range(hpg)]
    t2 = lax.broadcasted_iota(jnp.int32, (chunk, n), 0)
    s2 = lax.broadcasted_iota(jnp.int32, (chunk, n), 1) & (chunk - 1)
    strict = (s2 < t2).astype(F32)
    incl = (s2 <= t2).astype(F32)
    eye_row = (s2 == t2).astype(F32)
    rb = lax.broadcasted_iota(jnp.int32, (n, n), 0)
    cb = lax.broadcasted_iota(jnp.int32, (n, n), 1)
    bmask = ((rb >> sh) == (cb >> sh)).astype(F32)
    return op_masks, strict, incl, eye_row, bmask


def _load_state(s_ref, seq, gi, hpg):
    return jnp.concatenate([s_ref[seq, gi * hpg + e] for e in range(hpg)], axis=1)


def _store_state(s_ref, seq, gi, hpg, s_cat):
    for e in range(hpg):
        s_ref[seq, gi * hpg + e] = s_cat[:, e * HEAD_DIM:(e + 1) * HEAD_DIM]


def _wkv_prepass(r_ref, lw_ref, k_ref, v_ref, kk_ref, ba_ref, ops, clast_ref, chunk):
    at_ref, rt_ref, bt_ref, kt_ref, bh_ref, kh_ref, vb_ref = ops
    nr, c_dim = at_ref.shape
    dt = at_ref.dtype
    flat = lambda ref: ref[...].astype(F32).reshape(nr, c_dim)
    if dt == BF16:
        scaled = lambda ref, factor: ref[...].reshape(nr, c_dim) * factor.astype(BF16)
    else:
        scaled = lambda ref, factor: flat(ref) * factor
    sh = int(math.log2(chunk))
    ti = lax.broadcasted_iota(jnp.int32, (nr, nr), 0)
    si = lax.broadcasted_iota(jnp.int32, (nr, nr), 1)
    tri = (((ti >> sh) == (si >> sh)) & (ti >= si)).astype(BF16)
    lw = flat(lw_ref)
    c = _sum([jnp.dot(tri, q, preferred_element_type=F32) for q in _split2(lw)])
    c_last = jnp.concatenate(
        [jnp.broadcast_to(c[q + chunk - 1:q + chunk, :], (chunk, c_dim)) for q in range(0, nr, chunk)], axis=0)
    at_ref[...] = scaled(kk_ref, -jnp.exp(c - lw))
    rt_ref[...] = scaled(r_ref, jnp.exp(c))
    e_inv = jnp.exp(-c)
    bt_ref[...] = scaled(ba_ref, e_inv)
    kt_ref[...] = scaled(k_ref, e_inv)
    e_hat = jnp.exp(c_last - c)
    bh_ref[...] = scaled(ba_ref, e_hat)
    kh_ref[...] = scaled(k_ref, e_hat)
    vb_ref[...] = v_ref[...].reshape(nr, c_dim) if dt == BF16 else flat(v_ref)
    clast_ref[...] = c_last


def _wkv_chunk_step(states, row_starts, ops, clast_ref, masks, chunk, hpg, groups):
    at_ref, rt_ref, bt_ref, kt_ref, bh_ref, kh_ref, vb_ref = ops
    op_masks, strict, incl, eye_row, bmask = masks
    low2 = jnp.concatenate([strict, incl], axis=0)
    w = HEAD_DIM * hpg
    chains = [(s, g) for s in range(len(row_starts)) for g in range(groups)]

    def tile(ref, s, g):
        return ref[pl.ds(row_starts[s], chunk), g * w:(g + 1) * w]

    def stack(m):
        return jnp.concatenate([m.astype(lm.dtype) * lm for lm in op_masks], axis=0)

    def block_diag(m):
        return jnp.concatenate([m] * hpg, axis=0) * bmask

    lhs2 = [jnp.concatenate([tile(at_ref, s, g), tile(rt_ref, s, g)], axis=0) for s, g in chains]
    v_t = [tile(vb_ref, s, g) for s, g in chains]
    v_st = [stack(x) for x in v_t]
    n = hpg * chunk
    sc = [_bdot_nt(l, jnp.concatenate([stack(tile(bt_ref, s, g)), stack(tile(kt_ref, s, g))], axis=0))
          for l, (s, g) in zip(lhs2, chains)]
    sr = [_bdot_nt(l, stack(states[s][g])) for l, (s, g) in zip(lhs2, chains)]

    x_ab = [m[:chunk, :n] * strict for m in sc]
    x_rb = [m[chunk:, :n] * incl for m in sc]

    pw = [_bdot(a, block_diag(a)) for a in x_ab]
    srk = [m + _bdot(q[:, n:] * low2, vs) for m, q, vs in zip(sr, sc, v_st)]
    xa = [m[:chunk] for m in srk]
    y_k = [m[chunk:] for m in srk]
    t_inv = [eye_row + a for a in x_ab]
    for _ in range(int(math.log2(chunk)) - 2):
        t_new = [t + _bdot(p, block_diag(t)) for t, p in zip(t_inv, pw)]
        pw = [_bdot(p, block_diag(p)) for p in pw]
        t_inv = t_new
    t_inv = [t + _bdot(p, block_diag(t)) for t, p in zip(t_inv, pw)]

    u = [_bdot(t, stack(x)) for t, x in zip(t_inv, xa)]
    y = [yk + _bdot(x, stack(uu)) for yk, x, uu in zip(y_k, x_rb, u)]
    upd = []
    for uu, vv, (s, g) in zip(u, v_t, chains):
        uv = jnp.concatenate([uu, vv.astype(F32)], axis=0)
        uv_heads = jnp.concatenate([uv[:, e * HEAD_DIM:(e + 1) * HEAD_DIM] for e in range(hpg)], axis=0)
        bk = stack(jnp.concatenate([tile(bh_ref, s, g), tile(kh_ref, s, g)], axis=0))
        upd.append(_bdot_tn(uv_heads, bk))

    new_states = [[None] * groups for _ in row_starts]
    ys = [[None] * groups for _ in row_starts]
    for (s, g), up, yy in zip(chains, upd, y):
        decay = jnp.exp(clast_ref[pl.ds(row_starts[s], 1), g * w:(g + 1) * w])
        new_states[s][g] = states[s][g] * decay + up
        ys[s][g] = yy
    return new_states, ys


def _wkv_kernel(r_ref, lw_ref, k_ref, v_ref, kk_ref, ba_ref, s_in_ref, y_ref, s_out_ref, *scratch,
                chunk, hpg, groups, nseq, rows, long_mode):
    ops = scratch[:7]
    clast_ref = scratch[7]
    w = HEAD_DIM * hpg
    _wkv_prepass(r_ref, lw_ref, k_ref, v_ref, kk_ref, ba_ref, ops, clast_ref, chunk)
    masks = _wkv_masks(chunk, hpg, ops[0].dtype)

    if long_mode:
        state_ref = scratch[8]
        t = pl.program_id(1)

        @pl.when(t == 0)
        def _():
            for s in range(nseq):
                for g in range(groups):
                    state_ref[s, g] = _load_state(s_in_ref, s, g, hpg)

        def body(ci, carry):
            starts = [pl.multiple_of(s * rows + ci * chunk, chunk) for s in range(nseq)]
            states = [[state_ref[s, g] for g in range(groups)] for s in range(nseq)]
            new_states, ys = _wkv_chunk_step(states, starts, ops, clast_ref, masks, chunk, hpg, groups)
            for s in range(nseq):
                for g in range(groups):
                    state_ref[s, g] = new_states[s][g]
                    y_ref[s, pl.ds(pl.multiple_of(ci * chunk, chunk), chunk), g * w:(g + 1) * w] = ys[s][g]
            return carry

        lax.fori_loop(0, rows // chunk, body, 0)

        @pl.when(t == pl.num_programs(1) - 1)
        def _():
            for s in range(nseq):
                for g in range(groups):
                    _store_state(s_out_ref, s, g, hpg, state_ref[s, g])
    else:
        starts = [s * rows for s in range(nseq)]
        states = [[_load_state(s_in_ref, s, g, hpg) for g in range(groups)] for s in range(nseq)]
        new_states, ys = _wkv_chunk_step(states, starts, ops, clast_ref, masks, chunk, hpg, groups)
        for s in range(nseq):
            for g in range(groups):
                _store_state(s_out_ref, s, g, hpg, new_states[s][g])
                y_ref[s, :, g * w:(g + 1) * w] = ys[s][g]


def _wkv(r, lw, kh, v, kk, ba, s_all, layer, seq_len):
    total, c = r.shape
    nb = total // seq_len
    hpg = WKV_HEADS_PER_GROUP
    groups = c // (HEAD_DIM * hpg)
    n_heads = c // HEAD_DIM
    long_mode = seq_len > WKV_CHUNK
    if long_mode:
        chunk, nseq, rows = WKV_CHUNK, WKV_SEQS_PER_STEP, min(WKV_ROW_TILE, seq_len)
        tps = seq_len // rows
        grid = (nb // nseq, tps)
        row_spec = pl.BlockSpec((nseq, None, rows, c), lambda b, t: (b, t, 0, 0))
        s_spec = pl.BlockSpec((None, nseq, n_heads, HEAD_DIM, HEAD_DIM), lambda b, t: (layer, b, 0, 0, 0))
        shape4 = (nb, tps, rows, c)
        op_dtype = BF16
        sem = ("arbitrary", "arbitrary")
    else:
        chunk, nseq, rows = seq_len, WKV_SHORT_SEQS_PER_STEP, seq_len
        grid = (nb // nseq,)
        row_spec = pl.BlockSpec((nseq, None, rows, c), lambda b: (b, 0, 0, 0))
        s_spec = pl.BlockSpec((None, nseq, n_heads, HEAD_DIM, HEAD_DIM), lambda b: (layer, b, 0, 0, 0))
        shape4 = (nb, 1, rows, c)
        op_dtype = F32
        sem = ("arbitrary",)
    scratch = [pltpu.VMEM((nseq * rows, c), op_dtype)] * 7 + [pltpu.VMEM((nseq * rows, c), F32)]
    if long_mode:
        scratch.append(pltpu.VMEM((nseq, groups, HEAD_DIM, HEAD_DIM * hpg), F32))
    ins = [a.reshape(shape4) for a in (r, lw, kh, v, kk, ba)]
    y, s_out = pl.pallas_call(
        functools.partial(_wkv_kernel, chunk=chunk, hpg=hpg, groups=groups, nseq=nseq, rows=rows,
                          long_mode=long_mode),
        grid=grid, in_specs=[row_spec] * 6 + [s_spec], out_specs=[row_spec, s_spec],
        out_shape=[jax.ShapeDtypeStruct(shape4, F32), jax.ShapeDtypeStruct(s_all.shape, F32)],
        input_output_aliases={len(ins): 1},
        scratch_shapes=scratch, compiler_params=_params(sem), name="wkv7_chunked",
    )(*ins, s_all)
    return y.reshape(total, c), s_out


def _rwkv_out_kernel(y_ref, bonus_ref, sz_ref, x_ref, gng_ref, gnb_ref, wo_ref, ones_ref, o_ref):
    y = y_ref[...]
    inv_n = 1.0 / HEAD_DIM
    mean = _head_sum(y, ones_ref) * inv_n
    d = y - mean
    var = _head_sum(d * d, ones_ref) * inv_n
    yn = d * lax.rsqrt(var + GN_EPS) * gng_ref[...] + gnb_ref[...]
    gated = (yn + bonus_ref[...].astype(F32)) * sz_ref[...].astype(F32)
    o_ref[...] = x_ref[...] + _bdot(gated, wo_ref[...])


def _rwkv_out(y, bonus, sz, x2d, p):
    rows, c = y.shape
    d = x2d.shape[1]
    tile = min(OUT_ROW_TILE, rows)
    row_c = pl.BlockSpec((tile, c), lambda i: (i, 0))
    row_d = pl.BlockSpec((tile, d), lambda i: (i, 0))
    consts = [p['gn_g'], p['gn_b'], p['w_o'], p['ones']]
    return pl.pallas_call(
        _rwkv_out_kernel, grid=(rows // tile,),
        in_specs=[row_c, row_c, row_c, row_d] + [_const_spec(a.shape) for a in consts],
        out_specs=row_d, out_shape=jax.ShapeDtypeStruct((rows, d), F32),
        compiler_params=_params(("arbitrary",)), name="rwkv_out",
    )(y, bonus, sz, x2d, *consts)


def _pool_kernel(*refs, seq_len, tile, long_mode, pos0, final):
    it = iter(refs)
    x_ref, buf_ref, norm_ref, win_ref, wgrp_ref, bgrp_ref, scale_ref, wo_ref = (next(it) for _ in range(8))
    fnorm_ref = next(it) if final else None
    o_ref, nbuf_ref = next(it), next(it)
    u_ref, ext_ref, p_ref = next(it), next(it), next(it)
    carry_ref = next(it) if long_mode else None

    c = u_ref.shape[1]
    pg = c // len(POOL_WINDOWS)
    tl = tile if long_mode else seq_len
    x = x_ref[...]
    xn = _rms_norm(x, norm_ref[...])
    uz = _bdot(xn, win_ref[...])
    u_ref[...] = uz[:, :c]
    z = uz[:, c:]

    if long_mode:
        t = pl.program_id(1)
        row0 = t * tile
    else:
        row0 = 0
    pos = pos0 + row0 + lax.broadcasted_iota(jnp.int32, (tl, 1), 0)

    nbuf_rows = nbuf_ref.shape[1]

    def pool_one(base):
        ext_ref[POOL_HALO:POOL_HALO + tl, :] = u_ref[pl.ds(base, tl), :]
        for g, win in enumerate(POOL_WINDOWS):
            lanes = slice(g * pg, (g + 1) * pg)
            s = ext_ref[:, lanes]
            dlt = 1
            while dlt < win:
                s = s + pltpu.roll(s, dlt, 0)
                dlt *= 2
            cnt = jnp.minimum(win, pos + 1).astype(F32)
            cur = ext_ref[POOL_HALO:POOL_HALO + tl, lanes]
            p_ref[pl.ds(base, tl), lanes] = s[POOL_HALO:POOL_HALO + tl, :] / cnt - cur

    def load_halo(seq):
        ext_ref[0:POOL_HALO - nbuf_rows, :] = jnp.zeros((POOL_HALO - nbuf_rows, c), F32)
        ext_ref[POOL_HALO - nbuf_rows:POOL_HALO, :] = buf_ref[seq]

    if long_mode:
        @pl.when(t == 0)
        def _():
            load_halo(0)

        @pl.when(t > 0)
        def _():
            ext_ref[0:POOL_HALO, :] = carry_ref[...]

        pool_one(0)
        carry_ref[...] = ext_ref[tl:tl + POOL_HALO, :]
        nbuf_ref[0] = ext_ref[tl + POOL_HALO - nbuf_rows:tl + POOL_HALO, :]
    else:
        def body(s, carry):
            load_halo(s)
            pool_one(pl.multiple_of(s * seq_len, seq_len))
            nbuf_ref[s] = ext_ref[tl + POOL_HALO - nbuf_rows:tl + POOL_HALO, :]
            return carry

        lax.fori_loop(0, tile // seq_len, body, 0)

    p = p_ref[...]
    mixed = jnp.concatenate(
        [_bdot(p[:, g * pg:(g + 1) * pg], wgrp_ref[g]) for g in range(len(POOL_WINDOWS))], axis=1)
    mixed = (mixed + bgrp_ref[...]) * scale_ref[...]
    gated = mixed * (z * _sigmoid(z))
    out = x + _bdot(gated, wo_ref[...])
    if final:
        out = _rms_norm(out, fnorm_ref[...])
    o_ref[...] = out


def _pool_layer(x2d, buf_all, layer, seq_len, pos0, p, final_norm):
    rows, d = x2d.shape
    c = p['w_o'].shape[0]
    nb = rows // seq_len
    long_mode = seq_len >= POOL_ROW_TILE
    final = final_norm is not None
    nbuf_rows = buf_all.shape[2]
    if long_mode:
        tile = POOL_ROW_TILE
        tps = seq_len // tile
        grid = (nb, tps)
        row_spec = pl.BlockSpec((tile, d), lambda b, t: (b * tps + t, 0))
        buf_spec = pl.BlockSpec((None, 1, nbuf_rows, c), lambda b, t: (layer, b, 0, 0))
        tl = tile
        sem = ("arbitrary", "arbitrary")
    else:
        tile = min(POOL_ROW_TILE, rows)
        grid = (rows // tile,)
        row_spec = pl.BlockSpec((tile, d), lambda i: (i, 0))
        buf_spec = pl.BlockSpec((None, tile // seq_len, nbuf_rows, c), lambda i: (layer, i, 0, 0))
        tl = seq_len
        sem = ("arbitrary",)
    consts = [p['norm'], p['w_in'], p['w_grp'], p['b_grp'], p['scale'], p['w_o']]
    if final:
        consts.append(final_norm)
    scratch = [pltpu.VMEM((tile, c), F32), pltpu.VMEM((POOL_HALO + tl, c), F32), pltpu.VMEM((tile, c), F32)]
    if long_mode:
        scratch.append(pltpu.VMEM((POOL_HALO, c), F32))
    return pl.pallas_call(
        functools.partial(_pool_kernel, seq_len=seq_len, tile=tile, long_mode=long_mode, pos0=pos0,
                          final=final),
        grid=grid, in_specs=[row_spec, buf_spec] + [_const_spec(a.shape) for a in consts],
        out_specs=[row_spec, buf_spec],
        out_shape=[jax.ShapeDtypeStruct((rows, d), F32), jax.ShapeDtypeStruct(buf_all.shape, F32)],
        input_output_aliases={1: 1},
        scratch_shapes=scratch, compiler_params=_params(sem), name="pool_mixer",
    )(x2d, buf_all, *consts)


def _trunk(x, shift0, wkv0, buf0, pos0, rw, pw, vres, final_norm):
    nb, seq_len, d = x.shape
    x2d = x.reshape(nb * seq_len, d)
    new_shift = []
    v_first = None
    depth = len(rw) + len(pw)
    for i in range(depth):
        j = i // 2
        if i % 2 == 0:
            p = rw[j]
            if seq_len >= ROW_TILE:
                halo = shift0[j][:, None, :]
            else:
                halo = jnp.repeat(shift0[j], seq_len, axis=0)
            r, lw, kh, v, kk, ba, bonus, sz, xn_last = _rwkv_proj(
                x2d, halo, seq_len, p, None if j == 0 else vres[j - 1], v_first)
            if j == 0:
                v_first = v
            y, wkv0 = _wkv(r, lw, kh, v, kk, ba, wkv0, j, seq_len)
            x2d = _rwkv_out(y, bonus, sz, x2d, p)
            new_shift.append(xn_last.reshape(nb, -1, d)[:, -1, :])
        else:
            p = pw[j]
            x2d, buf0 = _pool_layer(x2d, buf0, j, seq_len, pos0, p,
                                    final_norm if i == depth - 1 else None)
    return x2d.reshape(nb, seq_len, d), jnp.stack(new_shift), wkv0, buf0


def kernel(x_prompt, x_sample, state_shift, state_wkv, state_pool, rwkv_norm, rwkv_mu, rwkv_w_r, rwkv_w_k, rwkv_w_v, rwkv_w_z, rwkv_w0, rwkv_w1, rwkv_w2, rwkv_a0, rwkv_a1, rwkv_a2, rwkv_v0, rwkv_v1, rwkv_v2, rwkv_k_k, rwkv_k_a, rwkv_r_k, rwkv_gn_g, rwkv_gn_b, rwkv_w_o, pool_norm, pool_w_in, pool_w_grp, pool_b_grp, pool_scale, pool_w_o, final_norm):
    n_rwkv, d, c = rwkv_w_r.shape
    n_pool = pool_w_in.shape[0]
    n_heads = c // HEAD_DIM
    bf = lambda a: a.astype(BF16)
    row = lambda a: a.reshape(1, -1)
    slab = min(c, HEAD_SUM_SLAB)
    head_of_lane = jnp.arange(slab, dtype=jnp.int32) // HEAD_DIM
    ones = (head_of_lane[:, None] == head_of_lane[None, :]).astype(BF16)
    rw = [dict(norm=row(rwkv_norm[j]), mu=rwkv_mu[j], w_r=bf(rwkv_w_r[j]), w_k=bf(rwkv_w_k[j]),
               w_v=bf(rwkv_w_v[j]), w_z=bf(rwkv_w_z[j]), w0=row(rwkv_w0[j]), w1=bf(rwkv_w1[j]),
               w2=bf(rwkv_w2[j]), a0=row(rwkv_a0[j]), a1=bf(rwkv_a1[j]), a2=bf(rwkv_a2[j]),
               k_k=row(rwkv_k_k[j]), k_a=row(rwkv_k_a[j]), r_k=row(rwkv_r_k[j]),
               gn_g=row(rwkv_gn_g[j]), gn_b=row(rwkv_gn_b[j]), w_o=bf(rwkv_w_o[j]), ones=ones)
          for j in range(n_rwkv)]
    vres = [(row(rwkv_v0[j]), bf(rwkv_v1[j]), bf(rwkv_v2[j])) for j in range(rwkv_v0.shape[0])]
    pw = [dict(norm=row(pool_norm[j]), w_in=bf(pool_w_in[j]), w_grp=bf(pool_w_grp[j]),
               b_grp=row(pool_b_grp[j]), scale=row(pool_scale[j]), w_o=bf(pool_w_o[j]))
          for j in range(n_pool)]
    fn = row(final_norm)

    nb = x_prompt.shape[0]
    shift0 = jnp.zeros((n_rwkv, nb, d), state_shift.dtype)
    wkv0 = jnp.zeros((n_rwkv, nb, n_heads, HEAD_DIM, HEAD_DIM), state_wkv.dtype)
    buf0 = jnp.zeros((n_pool, nb, state_pool.shape[2], c), state_pool.dtype)
    y_p, sh_p, wkv_p, pool_p = _trunk(x_prompt, shift0, wkv0, buf0, 0, rw, pw, vres, fn)
    y_s, sh_s, wkv_s, pool_s = _trunk(x_sample, state_shift, state_wkv, state_pool, PAST_LEN, rw, pw,
                                      vres, fn)
    return (y_p, y_s, sh_p, wkv_p, pool_p, sh_s, wkv_s, pool_s)
```

```python
import functools
import math

import jax
import jax.numpy as jnp
from jax import lax
from jax.experimental import pallas as pl
from jax.experimental.pallas import tpu as pltpu

F32 = jnp.float32
BF16 = jnp.bfloat16

HEAD_DIM = 64
POOL_WINDOWS = (2, 4, 8, 16)
POOL_HALO = 16
NORM_EPS = 1e-6
GN_EPS = 64e-5
PAST_LEN = 16384

VMEM_LIMIT_BYTES = 56 * 1024 * 1024
ROW_TILE = 256
PROJ_SLAB = 256
OUT_ROW_TILE = 256
POOL_ROW_TILE = 256
HEAD_SUM_SLAB = 256
WKV_ROW_TILE = 64
WKV_SEQS_PER_STEP = 2
WKV_SHORT_SEQS_PER_STEP = 4
WKV_CHUNK = 32
WKV_HEADS_PER_GROUP = 4


def _bdot(a, b):
    return jnp.dot(a.astype(BF16), b.astype(BF16), preferred_element_type=F32)


def _bdot_nt(a, b):
    return lax.dot_general(a.astype(BF16), b.astype(BF16), (((1,), (1,)), ((), ())),
                           preferred_element_type=F32)


def _bdot_tn(a, b):
    return lax.dot_general(a.astype(BF16), b.astype(BF16), (((0,), (0,)), ((), ())),
                           preferred_element_type=F32)


def _split2(x):
    hi = x.astype(BF16)
    lo = (x - hi.astype(F32)).astype(BF16)
    return hi, lo


def _head_sum(x, ones_ref):
    sel = ones_ref[...]
    b = sel.shape[0]
    xb = x.astype(BF16)
    return jnp.concatenate([jnp.dot(xb[:, i:i + b], sel, preferred_element_type=F32)
                            for i in range(0, x.shape[1], b)], axis=1)


def _sum(xs):
    return functools.reduce(lambda a, b: a + b, xs)


def _rms_norm(x, g):
    return x * lax.rsqrt(jnp.mean(x * x, axis=-1, keepdims=True) + NORM_EPS) * g


def _sigmoid(x):
    return 1.0 / (1.0 + jnp.exp(-x))


def _const_spec(shape):
    nd = len(shape)
    return pl.BlockSpec(shape, lambda *_: (0,) * nd, pipeline_mode=pl.Buffered(1))


def _params(sem):
    return pltpu.CompilerParams(dimension_semantics=sem, vmem_limit_bytes=VMEM_LIMIT_BYTES)


def _rwkv_proj_kernel(*refs, seq_len, tile, long_mode, has_vres):
    it = iter(refs)
    x_ref = next(it)
    halo_ref = next(it)
    norm_ref, mu_ref, wr_ref, wk_ref, wv_ref, wz_ref = (next(it) for _ in range(6))
    w0_ref, w1_ref, w2_ref, a0_ref, a1_ref, a2_ref = (next(it) for _ in range(6))
    kk_ref, ka_ref, rk_ref, ones_ref = (next(it) for _ in range(4))
    if has_vres:
        v0_ref, v1_ref, v2_ref, vfirst_ref = (next(it) for _ in range(4))
    r_o, lw_o, kh_o, v_o, kkn_o, ba_o, bonus_o, sz_o, xn_o = (next(it) for _ in range(9))
    carry_ref = next(it) if long_mode else None

    x = x_ref[...]
    xn = _rms_norm(x, norm_ref[...])
    rolled = pltpu.roll(xn, 1, 0)
    row = lax.broadcasted_iota(jnp.int32, xn.shape, 0)
    if long_mode:
        t = pl.program_id(1)

        @pl.when(t == 0)
        def _():
            carry_ref[...] = jnp.broadcast_to(halo_ref[0], carry_ref.shape)

        x_prev = jnp.where(row == 0, carry_ref[7:8, :], rolled)
        carry_ref[...] = xn[tile - 8:tile, :]
        xn_o[0] = xn[tile - 1:tile, :]
    else:
        x_prev = jnp.where((row & (seq_len - 1)) == 0, halo_ref[...], rolled)
        xn_o[...] = xn
    dx = x_prev - xn
    mu = mu_ref[...]
    xr, xw, xk, xv, xa, xg = ((xn + dx * mu[m:m + 1, :]).astype(BF16) for m in range(6))
    lora_w = jnp.tanh(_bdot(xw, w1_ref[...]))
    lora_a = _bdot(xa, a1_ref[...])
    if has_vres:
        lora_v = _bdot(xv, v1_ref[...])

    c = wr_ref.shape[1]
    slab = min(c, PROJ_SLAB)

    def slab_matmuls(j):
        sl = slice(j, j + slab)
        out = [_bdot(xr, wr_ref[:, sl]), _bdot(xk, wk_ref[:, sl]), _bdot(xv, wv_ref[:, sl]),
               _bdot(xg, wz_ref[:, sl]), _bdot(lora_w, w2_ref[:, sl]), _bdot(lora_a, a2_ref[:, sl])]
        if has_vres:
            out.append(_bdot(lora_v, v2_ref[:, sl]))
        return out

    starts = list(range(0, c, slab))
    ahead = slab_matmuls(starts[0])
    for idx, j in enumerate(starts):
        sl = slice(j, j + slab)
        r, k, v, z, w_lora, a_lora = ahead[:6]
        v_lora = ahead[6] if has_vres else None
        if idx + 1 < len(starts):
            ahead = slab_matmuls(starts[idx + 1])

        neg = -(w0_ref[:, sl] + w_lora)
        softplus = jnp.maximum(neg, 0.0) + jnp.log(1.0 + jnp.exp(-jnp.abs(neg)))
        lw_o[:, sl] = -jnp.exp(-softplus - 0.5)

        a = _sigmoid(a0_ref[:, sl] + a_lora)
        if has_vres:
            gate = _sigmoid(v0_ref[:, sl] + v_lora)
            v = v + (vfirst_ref[:, sl].astype(F32) - v) * gate

        kk = k * kk_ref[:, sl]
        nrm = jnp.maximum(jnp.sqrt(_head_sum(kk * kk, ones_ref)), 1e-12)
        kk = kk / nrm
        kh = k * (1.0 + (a - 1.0) * ka_ref[:, sl])
        bonus_o[:, sl] = (_head_sum(r * kh * rk_ref[:, sl], ones_ref) * v).astype(bonus_o.dtype)
        r_o[:, sl] = r.astype(r_o.dtype)
        kh_o[:, sl] = kh.astype(kh_o.dtype)
        v_o[:, sl] = v.astype(v_o.dtype)
        kkn_o[:, sl] = kk.astype(kkn_o.dtype)
        ba_o[:, sl] = (kk * a).astype(ba_o.dtype)
        sz_o[:, sl] = (z * _sigmoid(z)).astype(sz_o.dtype)


def _rwkv_proj(x2d, halo, seq_len, p, vres, vfirst):
    rows, d = x2d.shape
    c = p['w_r'].shape[1]
    nb = rows // seq_len
    long_mode = seq_len >= ROW_TILE
    tile = ROW_TILE if long_mode else min(ROW_TILE, rows)
    has_vres = vres is not None
    if long_mode:
        tps = seq_len // tile
        grid = (nb, tps)
        row_map = lambda b, t: (b * tps + t, 0)
        halo_spec = pl.BlockSpec((1, 1, d), lambda b, t: (b, 0, 0))
        xn_shape = jax.ShapeDtypeStruct((nb, 1, d), F32)
        xn_spec = pl.BlockSpec((1, 1, d), lambda b, t: (b, 0, 0))
        scratch = [pltpu.VMEM((8, d), F32)]
        sem = ("arbitrary", "arbitrary")
    else:
        grid = (rows // tile,)
        row_map = lambda i: (i, 0)
        halo_spec = pl.BlockSpec((tile, d), row_map)
        xn_shape = jax.ShapeDtypeStruct((rows, d), F32)
        xn_spec = pl.BlockSpec((tile, d), row_map)
        scratch = []
        sem = ("arbitrary",)
    row_d = pl.BlockSpec((tile, d), row_map)
    row_c = pl.BlockSpec((tile, c), row_map)
    consts = [p['norm'], p['mu'], p['w_r'], p['w_k'], p['w_v'], p['w_z'], p['w0'], p['w1'], p['w2'],
              p['a0'], p['a1'], p['a2'], p['k_k'], p['k_a'], p['r_k'], p['ones']]
    args = [x2d, halo] + consts
    in_specs = [row_d, halo_spec] + [_const_spec(a.shape) for a in consts]
    if has_vres:
        args += list(vres) + [vfirst]
        in_specs += [_const_spec(a.shape) for a in vres] + [row_c]
    out_dtypes = [BF16, F32, BF16, BF16, BF16, BF16, BF16, BF16]
    out_shape = [jax.ShapeDtypeStruct((rows, c), dt) for dt in out_dtypes] + [xn_shape]
    out_specs = [row_c] * 8 + [xn_spec]
    return pl.pallas_call(
        functools.partial(_rwkv_proj_kernel, seq_len=seq_len, tile=tile, long_mode=long_mode,
                          has_vres=has_vres),
        grid=grid, in_specs=in_specs, out_specs=out_specs, out_shape=out_shape,
        scratch_shapes=scratch, compiler_params=_params(sem), name="rwkv_proj",
    )(*args)


def _wkv_masks(chunk, hpg, op_dtype):
    w = HEAD_DIM * hpg
    n = chunk * hpg
    sh = int(math.log2(chunk))
    lane = lax.broadcasted_iota(jnp.int32, (1, w), 1)
    op_masks = [((lane >> 6) == e).astype(op_dtype) for e in range(hpg)]
    t2 = lax.broadcasted_iota(jnp.int32, (chunk, n), 0)
    s2 = lax.broadcasted_iota(jnp.int32, (chunk, n), 1) & (chunk - 1)
    strict = (s2 < t2).astype(F32)
    incl = (s2 <= t2).astype(F32)
    eye_row = (s2 == t2).astype(F32)
    rb = lax.broadcasted_iota(jnp.int32, (n, n), 0)
    cb = lax.broadcasted_iota(jnp.int32, (n, n), 1)
    bmask = ((rb >> sh) == (cb >> sh)).astype(F32)
    return op_masks, strict, incl, eye_row, bmask


def _load_state(s_ref, seq, gi, hpg):
    return jnp.concatenate([s_ref[seq, gi * hpg + e] for e in range(hpg)], axis=1)


def _store_state(s_ref, seq, gi, hpg, s_cat):
    for e in range(hpg):
        s_ref[seq, gi * hpg + e] = s_cat[:, e * HEAD_DIM:(e + 1) * HEAD_DIM]


def _wkv_prepass(r_ref, lw_ref, k_ref, v_ref, kk_ref, ba_ref, ops, clast_ref, chunk):
    at_ref, rt_ref, bt_ref, kt_ref, bh_ref, kh_ref, vb_ref = ops
    nr, c_dim = at_ref.shape
    dt = at_ref.dtype
    flat = lambda ref: ref[...].astype(F32).reshape(nr, c_dim)
    if dt == BF16:
        scaled = lambda ref, factor: ref[...].reshape(nr, c_dim) * factor.astype(BF16)
    else:
        scaled = lambda ref, factor: flat(ref) * factor
    sh = int(math.log2(chunk))
    ti = lax.broadcasted_iota(jnp.int32, (nr, nr), 0)
    si = lax.broadcasted_iota(jnp.int32, (nr, nr), 1)
    tri = (((ti >> sh) == (si >> sh)) & (ti >= si)).astype(BF16)
    lw = flat(lw_ref)
    c = _sum([jnp.dot(tri, q, preferred_element_type=F32) for q in _split2(lw)])
    c_last = jnp.concatenate(
        [jnp.broadcast_to(c[q + chunk - 1:q + chunk, :], (chunk, c_dim)) for q in range(0, nr, chunk)], axis=0)
    at_ref[...] = scaled(kk_ref, -jnp.exp(c - lw))
    rt_ref[...] = scaled(r_ref, jnp.exp(c))
    e_inv = jnp.exp(-c)
    bt_ref[...] = scaled(ba_ref, e_inv)
    kt_ref[...] = scaled(k_ref, e_inv)
    e_hat = jnp.exp(c_last - c)
    bh_ref[...] = scaled(ba_ref, e_hat)
    kh_ref[...] = scaled(k_ref, e_hat)
    vb_ref[...] = v_ref[...].reshape(nr, c_dim) if dt == BF16 else flat(v_ref)
    clast_ref[...] = c_last


def _wkv_chunk_step(states, row_starts, ops, clast_ref, masks, chunk, hpg, groups):
    at_ref, rt_ref, bt_ref, kt_ref, bh_ref, kh_ref, vb_ref = ops
    op_masks, strict, incl, eye_row, bmask = masks
    low2 = jnp.concatenate([strict, incl], axis=0)
    w = HEAD_DIM * hpg
    chains = [(s, g) for s in range(len(row_starts)) for g in range(groups)]

    def tile(ref, s, g):
        return ref[pl.ds(row_starts[s], chunk), g * w:(g + 1) * w]

    def stack(m):
        return jnp.concatenate([m.astype(lm.dtype) * lm for lm in op_masks], axis=0)

    def block_diag(m):
        return jnp.concatenate([m] * hpg, axis=0) * bmask

    lhs2 = [jnp.concatenate([tile(at_ref, s, g), tile(rt_ref, s, g)], axis=0) for s, g in chains]
    v_t = [tile(vb_ref, s, g) for s, g in chains]
    v_st = [stack(x) for x in v_t]
    n = hpg * chunk
    sc = [_bdot_nt(l, jnp.concatenate([stack(tile(bt_ref, s, g)), stack(tile(kt_ref, s, g))], axis=0))
          for l, (s, g) in zip(lhs2, chains)]
    sr = [_bdot_nt(l, stack(states[s][g])) for l, (s, g) in zip(lhs2, chains)]

    x_ab = [m[:chunk, :n] * strict for m in sc]
    x_rb = [m[chunk:, :n] * incl for m in sc]

    pw = [_bdot(a, block_diag(a)) for a in x_ab]
    srk = [m + _bdot(q[:, n:] * low2, vs) for m, q, vs in zip(sr, sc, v_st)]
    xa = [m[:chunk] for m in srk]
    y_k = [m[chunk:] for m in srk]
    t_inv = [eye_row + a for a in x_ab]
    for _ in range(int(math.log2(chunk)) - 2):
        t_new = [t + _bdot(p, block_diag(t)) for t, p in zip(t_inv, pw)]
        pw = [_bdot(p, block_diag(p)) for p in pw]
        t_inv = t_new
    t_inv = [t + _bdot(p, block_diag(t)) for t, p in zip(t_inv, pw)]

    u = [_bdot(t, stack(x)) for t, x in zip(t_inv, xa)]
    y = [yk + _bdot(x, stack(uu)) for yk, x, uu in zip(y_k, x_rb, u)]
    upd = []
    for uu, vv, (s, g) in zip(u, v_t, chains):
        uv = jnp.concatenate([uu, vv.astype(F32)], axis=0)
        uv_heads = jnp.concatenate([uv[:, e * HEAD_DIM:(e + 1) * HEAD_DIM] for e in range(hpg)], axis=0)
        bk = stack(jnp.concatenate([tile(bh_ref, s, g), tile(kh_ref, s, g)], axis=0))
        upd.append(_bdot_tn(uv_heads, bk))

    new_states = [[None] * groups for _ in row_starts]
    ys = [[None] * groups for _ in row_starts]
    for (s, g), up, yy in zip(chains, upd, y):
        decay = jnp.exp(clast_ref[pl.ds(row_starts[s], 1), g * w:(g + 1) * w])
        new_states[s][g] = states[s][g] * decay + up
        ys[s][g] = yy
    return new_states, ys


def _wkv_kernel(r_ref, lw_ref, k_ref, v_ref, kk_ref, ba_ref, s_in_ref, *rest,
                chunk, hpg, groups, nseq, rows, long_mode, own_layer):
    if own_layer is None:
        rest = rest[1:]
    y_ref, s_out_ref, *scratch = rest
    if own_layer is not None:
        s_all_ref = s_out_ref
        s_out_ref = s_all_ref.at[own_layer]

        def clear_other_layers():
            for layer in range(s_all_ref.shape[0]):
                if layer != own_layer:
                    s_all_ref[layer] = jnp.zeros(s_all_ref.shape[1:], F32)
    ops = scratch[:7]
    clast_ref = scratch[7]
    w = HEAD_DIM * hpg
    _wkv_prepass(r_ref, lw_ref, k_ref, v_ref, kk_ref, ba_ref, ops, clast_ref, chunk)
    masks = _wkv_masks(chunk, hpg, ops[0].dtype)

    if long_mode:
        state_ref = scratch[8]
        t = pl.program_id(1)

        @pl.when(t == 0)
        def _():
            for s in range(nseq):
                for g in range(groups):
                    state_ref[s, g] = _load_state(s_in_ref, s, g, hpg)

        def body(ci, carry):
            starts = [pl.multiple_of(s * rows + ci * chunk, chunk) for s in range(nseq)]
            states = [[state_ref[s, g] for g in range(groups)] for s in range(nseq)]
            new_states, ys = _wkv_chunk_step(states, starts, ops, clast_ref, masks, chunk, hpg, groups)
            for s in range(nseq):
                for g in range(groups):
                    state_ref[s, g] = new_states[s][g]
                    y_ref[s, pl.ds(pl.multiple_of(ci * chunk, chunk), chunk), g * w:(g + 1) * w] = (
                        ys[s][g].astype(y_ref.dtype))
            return carry

        lax.fori_loop(0, rows // chunk, body, 0)

        @pl.when(t == pl.num_programs(1) - 1)
        def _():
            for s in range(nseq):
                for g in range(groups):
                    _store_state(s_out_ref, s, g, hpg, state_ref[s, g])
            if own_layer is not None:
                clear_other_layers()
    else:
        starts = [s * rows for s in range(nseq)]
        states = [[_load_state(s_in_ref, s, g, hpg) for g in range(groups)] for s in range(nseq)]
        new_states, ys = _wkv_chunk_step(states, starts, ops, clast_ref, masks, chunk, hpg, groups)
        for s in range(nseq):
            for g in range(groups):
                _store_state(s_out_ref, s, g, hpg, new_states[s][g])
                y_ref[s, :, g * w:(g + 1) * w] = ys[s][g].astype(y_ref.dtype)
        if own_layer is not None:
            clear_other_layers()


def _wkv(r, lw, kh, v, kk, ba, s_in, s_all, layer, n_layers, seq_len):
    total, c = r.shape
    nb = total // seq_len
    hpg = WKV_HEADS_PER_GROUP
    groups = c // (HEAD_DIM * hpg)
    n_heads = c // HEAD_DIM
    long_mode = seq_len > WKV_CHUNK
    first = s_all is None
    if long_mode:
        chunk, nseq, rows = WKV_CHUNK, WKV_SEQS_PER_STEP, min(WKV_ROW_TILE, seq_len)
        tps = seq_len // rows
        grid = (nb // nseq, tps)
        row_spec = pl.BlockSpec((nseq, None, rows, c), lambda b, t: (b, t, 0, 0))
        s_spec = pl.BlockSpec((nseq, n_heads, HEAD_DIM, HEAD_DIM), lambda b, t: (b, 0, 0, 0))
        s_out_map = lambda b, t: (0 if first else layer, b, 0, 0, 0)
        y_dtype = BF16
        shape4 = (nb, tps, rows, c)
        op_dtype = BF16
        sem = ("arbitrary", "arbitrary")
    else:
        chunk, nseq, rows = seq_len, WKV_SHORT_SEQS_PER_STEP, seq_len
        grid = (nb // nseq,)
        row_spec = pl.BlockSpec((nseq, None, rows, c), lambda b: (b, 0, 0, 0))
        s_spec = pl.BlockSpec((nseq, n_heads, HEAD_DIM, HEAD_DIM), lambda b: (b, 0, 0, 0))
        s_out_map = lambda b: (0 if first else layer, b, 0, 0, 0)
        y_dtype = F32
        shape4 = (nb, 1, rows, c)
        op_dtype = F32
        sem = ("arbitrary",)
    scratch = [pltpu.VMEM((nseq * rows, c), op_dtype)] * 7 + [pltpu.VMEM((nseq * rows, c), F32)]
    if long_mode:
        scratch.append(pltpu.VMEM((nseq, groups, HEAD_DIM, HEAD_DIM * hpg), F32))
    ins = [a.reshape(shape4) for a in (r, lw, kh, v, kk, ba)] + [s_in]
    in_specs = [row_spec] * 6 + [s_spec]
    s_out_spec = pl.BlockSpec((n_layers if first else None, nseq, n_heads, HEAD_DIM, HEAD_DIM), s_out_map)
    if not first:
        ins.append(s_all)
        in_specs.append(pl.BlockSpec(memory_space=pl.ANY))
    y, s_out = pl.pallas_call(
        functools.partial(_wkv_kernel, chunk=chunk, hpg=hpg, groups=groups, nseq=nseq, rows=rows,
                          long_mode=long_mode, own_layer=layer if first else None),
        grid=grid, in_specs=in_specs, out_specs=[row_spec, s_out_spec],
        out_shape=[jax.ShapeDtypeStruct(shape4, y_dtype),
                   jax.ShapeDtypeStruct((n_layers,) + s_in.shape, F32)],
        input_output_aliases={} if first else {len(ins) - 1: 1},
        scratch_shapes=scratch, compiler_params=_params(sem), name="wkv7_chunked",
    )(*ins)
    return y.reshape(total, c), s_out


def _rwkv_out_kernel(y_ref, bonus_ref, sz_ref, x_ref, gng_ref, gnb_ref, wo_ref, ones_ref, o_ref):
    y = y_ref[...].astype(F32)
    inv_n = 1.0 / HEAD_DIM
    mean = _head_sum(y, ones_ref) * inv_n
    d = y - mean
    var = _head_sum(d * d, ones_ref) * inv_n
    yn = d * lax.rsqrt(var + GN_EPS) * gng_ref[...] + gnb_ref[...]
    gated = (yn + bonus_ref[...].astype(F32)) * sz_ref[...].astype(F32)
    o_ref[...] = x_ref[...] + _bdot(gated, wo_ref[...])


def _rwkv_out(y, bonus, sz, x2d, p):
    rows, c = y.shape
    d = x2d.shape[1]
    tile = min(OUT_ROW_TILE, rows)
    row_c = pl.BlockSpec((tile, c), lambda i: (i, 0))
    row_d = pl.BlockSpec((tile, d), lambda i: (i, 0))
    consts = [p['gn_g'], p['gn_b'], p['w_o'], p['ones']]
    return pl.pallas_call(
        _rwkv_out_kernel, grid=(rows // tile,),
        in_specs=[row_c, row_c, row_c, row_d] + [_const_spec(a.shape) for a in consts],
        out_specs=row_d, out_shape=jax.ShapeDtypeStruct((rows, d), F32),
        compiler_params=_params(("arbitrary",)), name="rwkv_out",
    )(y, bonus, sz, x2d, *consts)


def _pool_kernel(*refs, seq_len, tile, long_mode, pos0, final):
    it = iter(refs)
    x_ref, buf_ref, norm_ref, win_ref, wgrp_ref, bgrp_ref, scale_ref, wo_ref = (next(it) for _ in range(8))
    fnorm_ref = next(it) if final else None
    o_ref, nbuf_ref = next(it), next(it)
    u_ref, ext_ref, p_ref = next(it), next(it), next(it)
    carry_ref = next(it) if long_mode else None

    c = u_ref.shape[1]
    pg = c // len(POOL_WINDOWS)
    tl = tile if long_mode else seq_len
    x = x_ref[...]
    xn = _rms_norm(x, norm_ref[...])
    xb = xn.astype(BF16)
    u_ref[...] = _bdot(xb, win_ref[:, :c])
    z = _bdot(xb, win_ref[:, c:])

    if long_mode:
        t = pl.program_id(1)
        row0 = t * tile
    else:
        row0 = 0
    pos = pos0 + row0 + lax.broadcasted_iota(jnp.int32, (tl, 1), 0)

    nbuf_rows = nbuf_ref.shape[1]

    def pool_one(base):
        ext_ref[POOL_HALO:POOL_HALO + tl, :] = u_ref[pl.ds(base, tl), :]
        for g, win in enumerate(POOL_WINDOWS):
            lanes = slice(g * pg, (g + 1) * pg)
            s = ext_ref[:, lanes]
            dlt = 1
            while dlt < win:
                s = s + pltpu.roll(s, dlt, 0)
                dlt *= 2
            cnt = jnp.minimum(win, pos + 1).astype(F32)
            cur = ext_ref[POOL_HALO:POOL_HALO + tl, lanes]
            p_ref[pl.ds(base, tl), lanes] = s[POOL_HALO:POOL_HALO + tl, :] / cnt - cur

    def load_halo(seq):
        ext_ref[0:POOL_HALO - nbuf_rows, :] = jnp.zeros((POOL_HALO - nbuf_rows, c), F32)
        ext_ref[POOL_HALO - nbuf_rows:POOL_HALO, :] = buf_ref[seq]

    if long_mode:
        @pl.when(t == 0)
        def _():
            load_halo(0)

        @pl.when(t > 0)
        def _():
            ext_ref[0:POOL_HALO, :] = carry_ref[...]

        pool_one(0)
        carry_ref[...] = ext_ref[tl:tl + POOL_HALO, :]
        nbuf_ref[0] = ext_ref[tl + POOL_HALO - nbuf_rows:tl + POOL_HALO, :]
    else:
        def body(s, carry):
            load_halo(s)
            pool_one(pl.multiple_of(s * seq_len, seq_len))
            nbuf_ref[s] = ext_ref[tl + POOL_HALO - nbuf_rows:tl + POOL_HALO, :]
            return carry

        lax.fori_loop(0, tile // seq_len, body, 0)

    p = p_ref[...]
    mixed = jnp.concatenate(
        [_bdot(p[:, g * pg:(g + 1) * pg], wgrp_ref[g]) for g in range(len(POOL_WINDOWS))], axis=1)
    mixed = (mixed + bgrp_ref[...]) * scale_ref[...]
    gated = mixed * (z * _sigmoid(z))
    out = x + _bdot(gated, wo_ref[...])
    if final:
        out = _rms_norm(out, fnorm_ref[...])
    o_ref[...] = out


def _pool_layer(x2d, buf_all, layer, seq_len, pos0, p, final_norm):
    rows, d = x2d.shape
    c = p['w_o'].shape[0]
    nb = rows // seq_len
    long_mode = seq_len >= POOL_ROW_TILE
    final = final_norm is not None
    nbuf_rows = buf_all.shape[2]
    if long_mode:
        tile = POOL_ROW_TILE
        tps = seq_len // tile
        grid = (nb, tps)
        row_spec = pl.BlockSpec((tile, d), lambda b, t: (b * tps + t, 0))
        buf_spec = pl.BlockSpec((None, 1, nbuf_rows, c), lambda b, t: (layer, b, 0, 0))
        tl = tile
        sem = ("arbitrary", "arbitrary")
    else:
        tile = min(POOL_ROW_TILE, rows)
        grid = (rows // tile,)
        row_spec = pl.BlockSpec((tile, d), lambda i: (i, 0))
        buf_spec = pl.BlockSpec((None, tile // seq_len, nbuf_rows, c), lambda i: (layer, i, 0, 0))
        tl = seq_len
        sem = ("arbitrary",)
    consts = [p['norm'], p['w_in'], p['w_grp'], p['b_grp'], p['scale'], p['w_o']]
    if final:
        consts.append(final_norm)
    scratch = [pltpu.VMEM((tile, c), F32), pltpu.VMEM((POOL_HALO + tl, c), F32), pltpu.VMEM((tile, c), F32)]
    if long_mode:
        scratch.append(pltpu.VMEM((POOL_HALO, c), F32))
    return pl.pallas_call(
        functools.partial(_pool_kernel, seq_len=seq_len, tile=tile, long_mode=long_mode, pos0=pos0,
                          final=final),
        grid=grid, in_specs=[row_spec, buf_spec] + [_const_spec(a.shape) for a in consts],
        out_specs=[row_spec, buf_spec],
        out_shape=[jax.ShapeDtypeStruct((rows, d), F32), jax.ShapeDtypeStruct(buf_all.shape, F32)],
        input_output_aliases={1: 1},
        scratch_shapes=scratch, compiler_params=_params(sem), name="pool_mixer",
    )(x2d, buf_all, *consts)


def _trunk(x, shift0, wkv0, buf0, pos0, rw, pw, vres, final_norm):
    nb, seq_len, d = x.shape
    x2d = x.reshape(nb * seq_len, d)
    new_shift = []
    wkv_new = None
    v_first = None
    depth = len(rw) + len(pw)
    for i in range(depth):
        j = i // 2
        if i % 2 == 0:
            p = rw[j]
            if seq_len >= ROW_TILE:
                halo = shift0[j][:, None, :]
            else:
                halo = jnp.repeat(shift0[j], seq_len, axis=0)
            r, lw, kh, v, kk, ba, bonus, sz, xn_last = _rwkv_proj(
                x2d, halo, seq_len, p, None if j == 0 else vres[j - 1], v_first)
            if j == 0:
                v_first = v
            y, wkv_new = _wkv(r, lw, kh, v, kk, ba, wkv0[j], wkv_new, j, len(rw), seq_len)
            x2d = _rwkv_out(y, bonus, sz, x2d, p)
            new_shift.append(xn_last.reshape(nb, -1, d)[:, -1, :])
        else:
            p = pw[j]
            x2d, buf0 = _pool_layer(x2d, buf0, j, seq_len, pos0, p,
                                    final_norm if i == depth - 1 else None)
    return x2d.reshape(nb, seq_len, d), jnp.stack(new_shift), wkv_new, buf0


def kernel(x_prompt, x_sample, state_shift, state_wkv, state_pool, rwkv_norm, rwkv_mu, rwkv_w_r, rwkv_w_k, rwkv_w_v, rwkv_w_z, rwkv_w0, rwkv_w1, rwkv_w2, rwkv_a0, rwkv_a1, rwkv_a2, rwkv_v0, rwkv_v1, rwkv_v2, rwkv_k_k, rwkv_k_a, rwkv_r_k, rwkv_gn_g, rwkv_gn_b, rwkv_w_o, pool_norm, pool_w_in, pool_w_grp, pool_b_grp, pool_scale, pool_w_o, final_norm):
    n_rwkv, d, c = rwkv_w_r.shape
    n_pool = pool_w_in.shape[0]
    n_heads = c // HEAD_DIM
    bf = lambda a: a.astype(BF16)
    row = lambda a: a.reshape(1, -1)
    slab = min(c, HEAD_SUM_SLAB)
    head_of_lane = jnp.arange(slab, dtype=jnp.int32) // HEAD_DIM
    ones = (head_of_lane[:, None] == head_of_lane[None, :]).astype(BF16)
    rw = [dict(norm=row(rwkv_norm[j]), mu=rwkv_mu[j], w_r=bf(rwkv_w_r[j]), w_k=bf(rwkv_w_k[j]),
               w_v=bf(rwkv_w_v[j]), w_z=bf(rwkv_w_z[j]), w0=row(rwkv_w0[j]), w1=bf(rwkv_w1[j]),
               w2=bf(rwkv_w2[j]), a0=row(rwkv_a0[j]), a1=bf(rwkv_a1[j]), a2=bf(rwkv_a2[j]),
               k_k=row(rwkv_k_k[j]), k_a=row(rwkv_k_a[j]), r_k=row(rwkv_r_k[j]),
               gn_g=row(rwkv_gn_g[j]), gn_b=row(rwkv_gn_b[j]), w_o=bf(rwkv_w_o[j]), ones=ones)
          for j in range(n_rwkv)]
    vres = [(row(rwkv_v0[j]), bf(rwkv_v1[j]), bf(rwkv_v2[j])) for j in range(rwkv_v0.shape[0])]
    pw = [dict(norm=row(pool_norm[j]), w_in=bf(pool_w_in[j]), w_grp=bf(pool_w_grp[j]),
               b_grp=row(pool_b_grp[j]), scale=row(pool_scale[j]), w_o=bf(pool_w_o[j]))
          for j in range(n_pool)]
    fn = row(final_norm)

    nb = x_prompt.shape[0]
    shift0 = jnp.zeros((n_rwkv, nb, d), state_shift.dtype)
    wkv0 = jnp.zeros((n_rwkv, nb, n_heads, HEAD_DIM, HEAD_DIM), state_wkv.dtype)
    buf0 = jnp.zeros((n_pool, nb, state_pool.shape[2], c), state_pool.dtype)
    y_p, sh_p, wkv_p, pool_p = _trunk(x_prompt, shift0, wkv0, buf0, 0, rw, pw, vres, fn)
    y_s, sh_s, wkv_s, pool_s = _trunk(x_sample, state_shift, state_wkv, state_pool, PAST_LEN, rw, pw,
                                      vres, fn)
    return (y_p, y_s, sh_p, wkv_p, pool_p, sh_s, wkv_s, pool_s)
```

```python
import functools
import math

import jax
import jax.numpy as jnp
from jax import lax
from jax.experimental import pallas as pl
from jax.experimental.pallas import tpu as pltpu

F32 = jnp.float32
BF16 = jnp.bfloat16

HEAD_DIM = 64
POOL_WINDOWS = (2, 4, 8, 16)
POOL_HALO = 16
NORM_EPS = 1e-6
GN_EPS = 64e-5
PAST_LEN = 16384

VMEM_LIMIT_BYTES = 56 * 1024 * 1024
ROW_TILE = 256
PROJ_SLAB = 256
OUT_ROW_TILE = 256
POOL_ROW_TILE = 512
POOL_SUBTILES = 2
POOL_SHORT_ROW_TILE = 256
HEAD_SUM_SLAB = 256
WKV_ROW_TILE = 128
WKV_SEQS_PER_STEP = 2
WKV_SHORT_SEQS_PER_STEP = 4
WKV_CHUNK = 32
WKV_HEADS_PER_GROUP = 4


def _bdot(a, b):
    return jnp.dot(a.astype(BF16), b.astype(BF16), preferred_element_type=F32)


def _bdot_nt(a, b):
    return lax.dot_general(a.astype(BF16), b.astype(BF16), (((1,), (1,)), ((), ())),
                           preferred_element_type=F32)


def _bdot_tn(a, b):
    return lax.dot_general(a.astype(BF16), b.astype(BF16), (((0,), (0,)), ((), ())),
                           preferred_element_type=F32)


def _split2(x):
    hi = x.astype(BF16)
    lo = (x - hi.astype(F32)).astype(BF16)
    return hi, lo


def _head_sum(x, ones_ref):
    sel = ones_ref[...]
    b = sel.shape[0]
    xb = x.astype(BF16)
    return jnp.concatenate([jnp.dot(xb[:, i:i + b], sel, preferred_element_type=F32)
                            for i in range(0, x.shape[1], b)], axis=1)


def _sum(xs):
    return functools.reduce(lambda a, b: a + b, xs)


def _rms_norm(x, g):
    return x * lax.rsqrt(jnp.mean(x * x, axis=-1, keepdims=True) + NORM_EPS) * g


def _sigmoid(x):
    return 1.0 / (1.0 + jnp.exp(-x))


def _const_spec(shape):
    nd = len(shape)
    return pl.BlockSpec(shape, lambda *_: (0,) * nd, pipeline_mode=pl.Buffered(1))


def _params(sem):
    return pltpu.CompilerParams(dimension_semantics=sem, vmem_limit_bytes=VMEM_LIMIT_BYTES)


def _rwkv_proj_kernel(*refs, seq_len, tile, long_mode, has_vres):
    it = iter(refs)
    x_ref = next(it)
    halo_ref = next(it)
    norm_ref, mu_ref, wr_ref, wk_ref, wv_ref, wz_ref = (next(it) for _ in range(6))
    w0_ref, w1_ref, w2_ref, a0_ref, a1_ref, a2_ref = (next(it) for _ in range(6))
    kk_ref, ka_ref, rk_ref, ones_ref = (next(it) for _ in range(4))
    if has_vres:
        v0_ref, v1_ref, v2_ref, vfirst_ref = (next(it) for _ in range(4))
    r_o, lw_o, kh_o, v_o, kkn_o, ba_o, bonus_o, sz_o, xn_o = (next(it) for _ in range(9))
    carry_ref = next(it) if long_mode else None

    x = x_ref[...]
    xn = _rms_norm(x, norm_ref[...])
    rolled = pltpu.roll(xn, 1, 0)
    row = lax.broadcasted_iota(jnp.int32, xn.shape, 0)
    if long_mode:
        t = pl.program_id(1)

        @pl.when(t == 0)
        def _():
            carry_ref[...] = jnp.broadcast_to(halo_ref[0], carry_ref.shape)

        x_prev = jnp.where(row == 0, carry_ref[7:8, :], rolled)
        carry_ref[...] = xn[tile - 8:tile, :]
        xn_o[0] = xn[tile - 1:tile, :]
    else:
        x_prev = jnp.where((row & (seq_len - 1)) == 0, halo_ref[...], rolled)
        xn_o[...] = xn
    dx = x_prev - xn
    mu = mu_ref[...]
    xr, xw, xk, xv, xa, xg = ((xn + dx * mu[m:m + 1, :]).astype(BF16) for m in range(6))
    lora_w = jnp.tanh(_bdot(xw, w1_ref[...]))
    lora_a = _bdot(xa, a1_ref[...])
    if has_vres:
        lora_v = _bdot(xv, v1_ref[...])

    c = wr_ref.shape[1]
    slab = min(c, PROJ_SLAB)

    def slab_matmuls(j):
        sl = slice(j, j + slab)
        out = [_bdot(xr, wr_ref[:, sl]), _bdot(xk, wk_ref[:, sl]), _bdot(xv, wv_ref[:, sl]),
               _bdot(xg, wz_ref[:, sl]), _bdot(lora_w, w2_ref[:, sl]), _bdot(lora_a, a2_ref[:, sl])]
        if has_vres:
            out.append(_bdot(lora_v, v2_ref[:, sl]))
        return out

    starts = list(range(0, c, slab))
    ahead = slab_matmuls(starts[0])
    for idx, j in enumerate(starts):
        sl = slice(j, j + slab)
        r, k, v, z, w_lora, a_lora = ahead[:6]
        v_lora = ahead[6] if has_vres else None
        if idx + 1 < len(starts):
            ahead = slab_matmuls(starts[idx + 1])

        neg = -(w0_ref[:, sl] + w_lora)
        softplus = jnp.maximum(neg, 0.0) + jnp.log(1.0 + jnp.exp(-jnp.abs(neg)))
        lw_o[:, sl] = -jnp.exp(-softplus - 0.5)

        a = _sigmoid(a0_ref[:, sl] + a_lora)
        if has_vres:
            gate = _sigmoid(v0_ref[:, sl] + v_lora)
            v = v + (vfirst_ref[:, sl].astype(F32) - v) * gate

        kk = k * kk_ref[:, sl]
        nrm = jnp.maximum(jnp.sqrt(_head_sum(kk * kk, ones_ref)), 1e-12)
        kk = kk / nrm
        kh = k * (1.0 + (a - 1.0) * ka_ref[:, sl])
        bonus_o[:, sl] = (_head_sum(r * kh * rk_ref[:, sl], ones_ref) * v).astype(bonus_o.dtype)
        r_o[:, sl] = r.astype(r_o.dtype)
        kh_o[:, sl] = kh.astype(kh_o.dtype)
        v_o[:, sl] = v.astype(v_o.dtype)
        kkn_o[:, sl] = kk.astype(kkn_o.dtype)
        ba_o[:, sl] = (kk * a).astype(ba_o.dtype)
        sz_o[:, sl] = (z * _sigmoid(z)).astype(sz_o.dtype)


def _rwkv_proj(x2d, halo, seq_len, p, vres, vfirst):
    rows, d = x2d.shape
    c = p['w_r'].shape[1]
    nb = rows // seq_len
    long_mode = seq_len >= ROW_TILE
    tile = ROW_TILE if long_mode else min(ROW_TILE, rows)
    has_vres = vres is not None
    if long_mode:
        tps = seq_len // tile
        grid = (nb, tps)
        row_map = lambda b, t: (b * tps + t, 0)
        halo_spec = pl.BlockSpec((1, 1, d), lambda b, t: (b, 0, 0))
        xn_shape = jax.ShapeDtypeStruct((nb, 1, d), F32)
        xn_spec = pl.BlockSpec((1, 1, d), lambda b, t: (b, 0, 0))
        scratch = [pltpu.VMEM((8, d), F32)]
        sem = ("arbitrary", "arbitrary")
    else:
        grid = (rows // tile,)
        row_map = lambda i: (i, 0)
        halo_spec = pl.BlockSpec((tile, d), row_map)
        xn_shape = jax.ShapeDtypeStruct((rows, d), F32)
        xn_spec = pl.BlockSpec((tile, d), row_map)
        scratch = []
        sem = ("arbitrary",)
    row_d = pl.BlockSpec((tile, d), row_map)
    row_c = pl.BlockSpec((tile, c), row_map)
    consts = [p['norm'], p['mu'], p['w_r'], p['w_k'], p['w_v'], p['w_z'], p['w0'], p['w1'], p['w2'],
              p['a0'], p['a1'], p['a2'], p['k_k'], p['k_a'], p['r_k'], p['ones']]
    args = [x2d, halo] + consts
    in_specs = [row_d, halo_spec] + [_const_spec(a.shape) for a in consts]
    if has_vres:
        args += list(vres) + [vfirst]
        in_specs += [_const_spec(a.shape) for a in vres] + [row_c]
    out_dtypes = [BF16, F32, BF16, BF16, BF16, BF16, BF16, BF16]
    out_shape = [jax.ShapeDtypeStruct((rows, c), dt) for dt in out_dtypes] + [xn_shape]
    out_specs = [row_c] * 8 + [xn_spec]
    return pl.pallas_call(
        functools.partial(_rwkv_proj_kernel, seq_len=seq_len, tile=tile, long_mode=long_mode,
                          has_vres=has_vres),
        grid=grid, in_specs=in_specs, out_specs=out_specs, out_shape=out_shape,
        scratch_shapes=scratch, compiler_params=_params(sem), name="rwkv_proj",
    )(*args)


def _wkv_masks(chunk, hpg, op_dtype):
    w = HEAD_DIM * hpg
    n = chunk * hpg
    sh = int(math.log2(chunk))
    lane = lax.broadcasted_iota(jnp.int32, (1, w), 1)
    op_masks = [((lane >> 6) == e).astype(op_dtype) for e in range(hpg)]
    t2 = lax.broadcasted_iota(jnp.int32, (chunk, n), 0)
    s2 = lax.broadcasted_iota(jnp.int32, (chunk, n), 1) & (chunk - 1)
    strict = (s2 < t2).astype(F32)
    incl = (s2 <= t2).astype(F32)
    eye_row = (s2 == t2).astype(F32)
    rb = lax.broadcasted_iota(jnp.int32, (n, n), 0)
    cb = lax.broadcasted_iota(jnp.int32, (n, n), 1)
    bmask = ((rb >> sh) == (cb >> sh)).astype(F32)
    return op_masks, strict, incl, eye_row, bmask


def _load_state(s_ref, seq, gi, hpg):
    return jnp.concatenate([s_ref[seq, gi * hpg + e] for e in range(hpg)], axis=1)


def _store_state(s_ref, seq, gi, hpg, s_cat):
    for e in range(hpg):
        s_ref[seq, gi * hpg + e] = s_cat[:, e * HEAD_DIM:(e + 1) * HEAD_DIM]


def _wkv_prepass(r_ref, lw_ref, k_ref, v_ref, kk_ref, ba_ref, ops, clast_ref, chunk):
    at_ref, rt_ref, bt_ref, kt_ref, bh_ref, kh_ref, vb_ref = ops
    nr, c_dim = at_ref.shape
    dt = at_ref.dtype
    flat = lambda ref: ref[...].astype(F32).reshape(nr, c_dim)
    if dt == BF16:
        scaled = lambda ref, factor: ref[...].reshape(nr, c_dim) * factor.astype(BF16)
    else:
        scaled = lambda ref, factor: flat(ref) * factor
    sh = int(math.log2(chunk))
    ti = lax.broadcasted_iota(jnp.int32, (nr, nr), 0)
    si = lax.broadcasted_iota(jnp.int32, (nr, nr), 1)
    tri = (((ti >> sh) == (si >> sh)) & (ti >= si)).astype(BF16)
    lw = flat(lw_ref)
    c = _sum([jnp.dot(tri, q, preferred_element_type=F32) for q in _split2(lw)])
    c_last = jnp.concatenate(
        [jnp.broadcast_to(c[q + chunk - 1:q + chunk, :], (chunk, c_dim)) for q in range(0, nr, chunk)], axis=0)
    at_ref[...] = scaled(kk_ref, -jnp.exp(c - lw))
    rt_ref[...] = scaled(r_ref, jnp.exp(c))
    e_inv = jnp.exp(-c)
    bt_ref[...] = scaled(ba_ref, e_inv)
    kt_ref[...] = scaled(k_ref, e_inv)
    e_hat = jnp.exp(c_last - c)
    bh_ref[...] = scaled(ba_ref, e_hat)
    kh_ref[...] = scaled(k_ref, e_hat)
    vb_ref[...] = v_ref[...].reshape(nr, c_dim) if dt == BF16 else flat(v_ref)
    clast_ref[...] = c_last


def _wkv_chunk_step(states, row_starts, ops, clast_ref, masks, chunk, hpg, groups):
    at_ref, rt_ref, bt_ref, kt_ref, bh_ref, kh_ref, vb_ref = ops
    op_masks, strict, incl, eye_row, bmask = masks
    low2 = jnp.concatenate([strict, incl], axis=0)
    w = HEAD_DIM * hpg
    chains = [(s, g) for s in range(len(row_starts)) for g in range(groups)]

    def tile(ref, s, g):
        return ref[pl.ds(row_starts[s], chunk), g * w:(g + 1) * w]

    def stack(m):
        return jnp.concatenate([m.astype(lm.dtype) * lm for lm in op_masks], axis=0)

    def block_diag(m):
        return jnp.concatenate([m] * hpg, axis=0) * bmask

    lhs2 = [jnp.concatenate([tile(at_ref, s, g), tile(rt_ref, s, g)], axis=0) for s, g in chains]
    v_t = [tile(vb_ref, s, g) for s, g in chains]
    v_st = [stack(x) for x in v_t]
    n = hpg * chunk
    sc = [_bdot_nt(l, jnp.concatenate([stack(tile(bt_ref, s, g)), stack(tile(kt_ref, s, g))], axis=0))
          for l, (s, g) in zip(lhs2, chains)]
    sr = [_bdot_nt(l, stack(states[s][g])) for l, (s, g) in zip(lhs2, chains)]

    x_ab = [m[:chunk, :n] * strict for m in sc]
    x_rb = [m[chunk:, :n] * incl for m in sc]

    pw = [_bdot(a, block_diag(a)) for a in x_ab]
    srk = [m + _bdot(q[:, n:] * low2, vs) for m, q, vs in zip(sr, sc, v_st)]
    xa = [m[:chunk] for m in srk]
    y_k = [m[chunk:] for m in srk]
    t_inv = [eye_row + a for a in x_ab]
    for _ in range(int(math.log2(chunk)) - 2):
        t_new = [t + _bdot(p, block_diag(t)) for t, p in zip(t_inv, pw)]
        pw = [_bdot(p, block_diag(p)) for p in pw]
        t_inv = t_new
    t_inv = [t + _bdot(p, block_diag(t)) for t, p in zip(t_inv, pw)]

    u = [_bdot(t, stack(x)) for t, x in zip(t_inv, xa)]
    y = [yk + _bdot(x, stack(uu)) for yk, x, uu in zip(y_k, x_rb, u)]
    upd = []
    for uu, vv, (s, g) in zip(u, v_t, chains):
        uv = jnp.concatenate([uu, vv.astype(F32)], axis=0)
        uv_heads = jnp.concatenate([uv[:, e * HEAD_DIM:(e + 1) * HEAD_DIM] for e in range(hpg)], axis=0)
        bk = stack(jnp.concatenate([tile(bh_ref, s, g), tile(kh_ref, s, g)], axis=0))
        upd.append(_bdot_tn(uv_heads, bk))

    new_states = [[None] * groups for _ in row_starts]
    ys = [[None] * groups for _ in row_starts]
    for (s, g), up, yy in zip(chains, upd, y):
        decay = jnp.exp(clast_ref[pl.ds(row_starts[s], 1), g * w:(g + 1) * w])
        new_states[s][g] = states[s][g] * decay + up
        ys[s][g] = yy
    return new_states, ys


def _wkv_kernel(r_ref, lw_ref, k_ref, v_ref, kk_ref, ba_ref, s_in_ref, *rest,
                chunk, hpg, groups, nseq, rows, long_mode, own_layer):
    if own_layer is None:
        rest = rest[1:]
    y_ref, s_out_ref, *scratch = rest
    if own_layer is not None:
        s_all_ref = s_out_ref
        s_out_ref = s_all_ref.at[own_layer]

        def clear_other_layers():
            for layer in range(s_all_ref.shape[0]):
                if layer != own_layer:
                    s_all_ref[layer] = jnp.zeros(s_all_ref.shape[1:], F32)
    ops = scratch[:7]
    clast_ref = scratch[7]
    w = HEAD_DIM * hpg
    _wkv_prepass(r_ref, lw_ref, k_ref, v_ref, kk_ref, ba_ref, ops, clast_ref, chunk)
    masks = _wkv_masks(chunk, hpg, ops[0].dtype)

    if long_mode:
        state_ref = scratch[8]
        t = pl.program_id(1)

        @pl.when(t == 0)
        def _():
            for s in range(nseq):
                for g in range(groups):
                    state_ref[s, g] = _load_state(s_in_ref, s, g, hpg)

        def body(ci, carry):
            starts = [pl.multiple_of(s * rows + ci * chunk, chunk) for s in range(nseq)]
            states = [[state_ref[s, g] for g in range(groups)] for s in range(nseq)]
            new_states, ys = _wkv_chunk_step(states, starts, ops, clast_ref, masks, chunk, hpg, groups)
            for s in range(nseq):
                for g in range(groups):
                    state_ref[s, g] = new_states[s][g]
                    y_ref[s, pl.ds(pl.multiple_of(ci * chunk, chunk), chunk), g * w:(g + 1) * w] = (
                        ys[s][g].astype(y_ref.dtype))
            return carry

        lax.fori_loop(0, rows // chunk, body, 0)

        @pl.when(t == pl.num_programs(1) - 1)
        def _():
            for s in range(nseq):
                for g in range(groups):
                    _store_state(s_out_ref, s, g, hpg, state_ref[s, g])
            if own_layer is not None:
                clear_other_layers()
    else:
        starts = [s * rows for s in range(nseq)]
        states = [[_load_state(s_in_ref, s, g, hpg) for g in range(groups)] for s in range(nseq)]
        new_states, ys = _wkv_chunk_step(states, starts, ops, clast_ref, masks, chunk, hpg, groups)
        for s in range(nseq):
            for g in range(groups):
                _store_state(s_out_ref, s, g, hpg, new_states[s][g])
                y_ref[s, :, g * w:(g + 1) * w] = ys[s][g].astype(y_ref.dtype)
        if own_layer is not None:
            clear_other_layers()


def _wkv(r, lw, kh, v, kk, ba, s_in, s_all, layer, n_layers, seq_len):
    total, c = r.shape
    nb = total // seq_len
    hpg = WKV_HEADS_PER_GROUP
    groups = c // (HEAD_DIM * hpg)
    n_heads = c // HEAD_DIM
    long_mode = seq_len > WKV_CHUNK
    first = s_all is None
    if long_mode:
        chunk, nseq, rows = WKV_CHUNK, WKV_SEQS_PER_STEP, min(WKV_ROW_TILE, seq_len)
        tps = seq_len // rows
        grid = (nb // nseq, tps)
        row_spec = pl.BlockSpec((nseq, None, rows, c), lambda b, t: (b, t, 0, 0))
        s_spec = pl.BlockSpec((nseq, n_heads, HEAD_DIM, HEAD_DIM), lambda b, t: (b, 0, 0, 0))
        s_out_map = lambda b, t: (0 if first else layer, b, 0, 0, 0)
        y_dtype = BF16
        shape4 = (nb, tps, rows, c)
        op_dtype = BF16
        sem = ("arbitrary", "arbitrary")
    else:
        chunk, nseq, rows = seq_len, WKV_SHORT_SEQS_PER_STEP, seq_len
        grid = (nb // nseq,)
        row_spec = pl.BlockSpec((nseq, None, rows, c), lambda b: (b, 0, 0, 0))
        s_spec = pl.BlockSpec((nseq, n_heads, HEAD_DIM, HEAD_DIM), lambda b: (b, 0, 0, 0))
        s_out_map = lambda b: (0 if first else layer, b, 0, 0, 0)
        y_dtype = F32
        shape4 = (nb, 1, rows, c)
        op_dtype = F32
        sem = ("arbitrary",)
    scratch = [pltpu.VMEM((nseq * rows, c), op_dtype)] * 7 + [pltpu.VMEM((nseq * rows, c), F32)]
    if long_mode:
        scratch.append(pltpu.VMEM((nseq, groups, HEAD_DIM, HEAD_DIM * hpg), F32))
    ins = [a.reshape(shape4) for a in (r, lw, kh, v, kk, ba)] + [s_in]
    in_specs = [row_spec] * 6 + [s_spec]
    s_out_spec = pl.BlockSpec((n_layers if first else None, nseq, n_heads, HEAD_DIM, HEAD_DIM), s_out_map)
    if not first:
        ins.append(s_all)
        in_specs.append(pl.BlockSpec(memory_space=pl.ANY))
    y, s_out = pl.pallas_call(
        functools.partial(_wkv_kernel, chunk=chunk, hpg=hpg, groups=groups, nseq=nseq, rows=rows,
                          long_mode=long_mode, own_layer=layer if first else None),
        grid=grid, in_specs=in_specs, out_specs=[row_spec, s_out_spec],
        out_shape=[jax.ShapeDtypeStruct(shape4, y_dtype),
                   jax.ShapeDtypeStruct((n_layers,) + s_in.shape, F32)],
        input_output_aliases={} if first else {len(ins) - 1: 1},
        scratch_shapes=scratch, compiler_params=_params(sem), name="wkv7_chunked",
    )(*ins)
    return y.reshape(total, c), s_out


def _rwkv_out_kernel(y_ref, bonus_ref, sz_ref, x_ref, gng_ref, gnb_ref, wo_ref, ones_ref, o_ref):
    y = y_ref[...].astype(F32)
    inv_n = 1.0 / HEAD_DIM
    mean = _head_sum(y, ones_ref) * inv_n
    d = y - mean
    var = _head_sum(d * d, ones_ref) * inv_n
    yn = d * lax.rsqrt(var + GN_EPS) * gng_ref[...] + gnb_ref[...]
    gated = (yn + bonus_ref[...].astype(F32)) * sz_ref[...].astype(F32)
    o_ref[...] = x_ref[...] + _bdot(gated, wo_ref[...])


def _rwkv_out(y, bonus, sz, x2d, p):
    rows, c = y.shape
    d = x2d.shape[1]
    tile = min(OUT_ROW_TILE, rows)
    row_c = pl.BlockSpec((tile, c), lambda i: (i, 0))
    row_d = pl.BlockSpec((tile, d), lambda i: (i, 0))
    consts = [p['gn_g'], p['gn_b'], p['w_o'], p['ones']]
    return pl.pallas_call(
        _rwkv_out_kernel, grid=(rows // tile,),
        in_specs=[row_c, row_c, row_c, row_d] + [_const_spec(a.shape) for a in consts],
        out_specs=row_d, out_shape=jax.ShapeDtypeStruct((rows, d), F32),
        compiler_params=_params(("arbitrary",)), name="rwkv_out",
    )(y, bonus, sz, x2d, *consts)


def _pool_kernel(*refs, seq_len, tile, long_mode, pos0, final):
    it = iter(refs)
    x_ref, buf_ref, norm_ref, win_ref, wgrp_ref, bgrp_ref, scale_ref, wo_ref = (next(it) for _ in range(8))
    fnorm_ref = next(it) if final else None
    o_ref, nbuf_ref = next(it), next(it)
    u_ref, ext_ref, p_ref = next(it), next(it), next(it)
    carry_ref = next(it) if long_mode else None

    c = u_ref.shape[1]
    pg = c // len(POOL_WINDOWS)
    n_sub = POOL_SUBTILES if long_mode else 1
    sub = tile // n_sub
    tl = sub if long_mode else seq_len
    nbuf_rows = nbuf_ref.shape[1]

    gates = []
    for h in range(n_sub):
        rows = slice(h * sub, (h + 1) * sub)
        xb = _rms_norm(x_ref[rows, :], norm_ref[...]).astype(BF16)
        u_ref[rows, :] = _bdot(xb, win_ref[:, :c])
        gates.append(_bdot(xb, win_ref[:, c:]))

    step = lax.broadcasted_iota(jnp.int32, (tl, 1), 0)

    def pool_one(base, pos):
        ext_ref[POOL_HALO:POOL_HALO + tl, :] = u_ref[pl.ds(base, tl), :]
        for g, win in enumerate(POOL_WINDOWS):
            lanes = slice(g * pg, (g + 1) * pg)
            s = ext_ref[:, lanes]
            dlt = 1
            while dlt < win:
                s = s + pltpu.roll(s, dlt, 0)
                dlt *= 2
            cnt = jnp.minimum(win, pos + 1).astype(F32)
            cur = ext_ref[POOL_HALO:POOL_HALO + tl, lanes]
            p_ref[pl.ds(base, tl), lanes] = s[POOL_HALO:POOL_HALO + tl, :] / cnt - cur

    def load_halo(seq):
        ext_ref[0:POOL_HALO - nbuf_rows, :] = jnp.zeros((POOL_HALO - nbuf_rows, c), F32)
        ext_ref[POOL_HALO - nbuf_rows:POOL_HALO, :] = buf_ref[seq]

    def finish(h):
        rows = slice(h * sub, (h + 1) * sub)
        p = p_ref[rows, :]
        mixed = jnp.concatenate(
            [_bdot(p[:, g * pg:(g + 1) * pg], wgrp_ref[g]) for g in range(len(POOL_WINDOWS))], axis=1)
        mixed = (mixed + bgrp_ref[...]) * scale_ref[...]
        z = gates[h]
        out = x_ref[rows, :] + _bdot(mixed * (z * _sigmoid(z)), wo_ref[...])
        if final:
            out = _rms_norm(out, fnorm_ref[...])
        o_ref[rows, :] = out

    if long_mode:
        t = pl.program_id(1)

        @pl.when(t == 0)
        def _():
            load_halo(0)

        @pl.when(t > 0)
        def _():
            ext_ref[0:POOL_HALO, :] = carry_ref[...]

        for h in range(n_sub):
            if h > 0:
                ext_ref[0:POOL_HALO, :] = ext_ref[tl:tl + POOL_HALO, :]
            pool_one(h * sub, pos0 + t * tile + h * sub + step)
            if h == n_sub - 1:
                carry_ref[...] = ext_ref[tl:tl + POOL_HALO, :]
                nbuf_ref[0] = ext_ref[tl + POOL_HALO - nbuf_rows:tl + POOL_HALO, :]
            finish(h)
    else:
        def body(s, carry):
            load_halo(s)
            pool_one(pl.multiple_of(s * seq_len, seq_len), pos0 + step)
            nbuf_ref[s] = ext_ref[tl + POOL_HALO - nbuf_rows:tl + POOL_HALO, :]
            return carry

        lax.fori_loop(0, tile // seq_len, body, 0)
        finish(0)


def _pool_layer(x2d, buf_all, layer, seq_len, pos0, p, final_norm):
    rows, d = x2d.shape
    c = p['w_o'].shape[0]
    nb = rows // seq_len
    long_mode = seq_len >= POOL_ROW_TILE
    final = final_norm is not None
    nbuf_rows = buf_all.shape[2]
    if long_mode:
        tile = POOL_ROW_TILE
        tps = seq_len // tile
        grid = (nb, tps)
        row_spec = pl.BlockSpec((tile, d), lambda b, t: (b * tps + t, 0))
        buf_spec = pl.BlockSpec((None, 1, nbuf_rows, c), lambda b, t: (layer, b, 0, 0))
        tl = tile // POOL_SUBTILES
        sem = ("arbitrary", "arbitrary")
    else:
        tile = min(POOL_SHORT_ROW_TILE, rows)
        grid = (rows // tile,)
        row_spec = pl.BlockSpec((tile, d), lambda i: (i, 0))
        buf_spec = pl.BlockSpec((None, tile // seq_len, nbuf_rows, c), lambda i: (layer, i, 0, 0))
        tl = seq_len
        sem = ("arbitrary",)
    consts = [p['norm'], p['w_in'], p['w_grp'], p['b_grp'], p['scale'], p['w_o']]
    if final:
        consts.append(final_norm)
    scratch = [pltpu.VMEM((tile, c), F32), pltpu.VMEM((POOL_HALO + tl, c), F32), pltpu.VMEM((tile, c), F32)]
    if long_mode:
        scratch.append(pltpu.VMEM((POOL_HALO, c), F32))
    return pl.pallas_call(
        functools.partial(_pool_kernel, seq_len=seq_len, tile=tile, long_mode=long_mode, pos0=pos0,
                          final=final),
        grid=grid, in_specs=[row_spec, buf_spec] + [_const_spec(a.shape) for a in consts],
        out_specs=[row_spec, buf_spec],
        out_shape=[jax.ShapeDtypeStruct((rows, d), F32), jax.ShapeDtypeStruct(buf_all.shape, F32)],
        input_output_aliases={1: 1},
        scratch_shapes=scratch, compiler_params=_params(sem), name="pool_mixer",
    )(x2d, buf_all, *consts)


def _trunk(x, shift0, wkv0, buf0, pos0, rw, pw, vres, final_norm):
    nb, seq_len, d = x.shape
    x2d = x.reshape(nb * seq_len, d)
    new_shift = []
    wkv_new = None
    v_first = None
    depth = len(rw) + len(pw)
    for i in range(depth):
        j = i // 2
        if i % 2 == 0:
            p = rw[j]
            if seq_len >= ROW_TILE:
                halo = shift0[j][:, None, :]
            else:
                halo = jnp.repeat(shift0[j], seq_len, axis=0)
            r, lw, kh, v, kk, ba, bonus, sz, xn_last = _rwkv_proj(
                x2d, halo, seq_len, p, None if j == 0 else vres[j - 1], v_first)
            if j == 0:
                v_first = v
            y, wkv_new = _wkv(r, lw, kh, v, kk, ba, wkv0[j], wkv_new, j, len(rw), seq_len)
            x2d = _rwkv_out(y, bonus, sz, x2d, p)
            new_shift.append(xn_last.reshape(nb, -1, d)[:, -1, :])
        else:
            p = pw[j]
            x2d, buf0 = _pool_layer(x2d, buf0, j, seq_len, pos0, p,
                                    final_norm if i == depth - 1 else None)
    return x2d.reshape(nb, seq_len, d), jnp.stack(new_shift), wkv_new, buf0


def kernel(x_prompt, x_sample, state_shift, state_wkv, state_pool, rwkv_norm, rwkv_mu, rwkv_w_r, rwkv_w_k, rwkv_w_v, rwkv_w_z, rwkv_w0, rwkv_w1, rwkv_w2, rwkv_a0, rwkv_a1, rwkv_a2, rwkv_v0, rwkv_v1, rwkv_v2, rwkv_k_k, rwkv_k_a, rwkv_r_k, rwkv_gn_g, rwkv_gn_b, rwkv_w_o, pool_norm, pool_w_in, pool_w_grp, pool_b_grp, pool_scale, pool_w_o, final_norm):
    n_rwkv, d, c = rwkv_w_r.shape
    n_pool = pool_w_in.shape[0]
    n_heads = c // HEAD_DIM
    bf = lambda a: a.astype(BF16)
    row = lambda a: a.reshape(1, -1)
    slab = min(c, HEAD_SUM_SLAB)
    head_of_lane = jnp.arange(slab, dtype=jnp.int32) // HEAD_DIM
    ones = (head_of_lane[:, None] == head_of_lane[None, :]).astype(BF16)
    rw = [dict(norm=row(rwkv_norm[j]), mu=rwkv_mu[j], w_r=bf(rwkv_w_r[j]), w_k=bf(rwkv_w_k[j]),
               w_v=bf(rwkv_w_v[j]), w_z=bf(rwkv_w_z[j]), w0=row(rwkv_w0[j]), w1=bf(rwkv_w1[j]),
               w2=bf(rwkv_w2[j]), a0=row(rwkv_a0[j]), a1=bf(rwkv_a1[j]), a2=bf(rwkv_a2[j]),
               k_k=row(rwkv_k_k[j]), k_a=row(rwkv_k_a[j]), r_k=row(rwkv_r_k[j]),
               gn_g=row(rwkv_gn_g[j]), gn_b=row(rwkv_gn_b[j]), w_o=bf(rwkv_w_o[j]), ones=ones)
          for j in range(n_rwkv)]
    vres = [(row(rwkv_v0[j]), bf(rwkv_v1[j]), bf(rwkv_v2[j])) for j in range(rwkv_v0.shape[0])]
    pw = [dict(norm=row(pool_norm[j]), w_in=bf(pool_w_in[j]), w_grp=bf(pool_w_grp[j]),
               b_grp=row(pool_b_grp[j]), scale=row(pool_scale[j]), w_o=bf(pool_w_o[j]))
          for j in range(n_pool)]
    fn = row(final_norm)

    nb = x_prompt.shape[0]
    shift0 = jnp.zeros((n_rwkv, nb, d), state_shift.dtype)
    wkv0 = jnp.zeros((n_rwkv, nb, n_heads, HEAD_DIM, HEAD_DIM), state_wkv.dtype)
    buf0 = jnp.zeros((n_pool, nb, state_pool.shape[2], c), state_pool.dtype)
    y_p, sh_p, wkv_p, pool_p = _trunk(x_prompt, shift0, wkv0, buf0, 0, rw, pw, vres, fn)
    y_s, sh_s, wkv_s, pool_s = _trunk(x_sample, state_shift, state_wkv, state_pool, PAST_LEN, rw, pw,
                                      vres, fn)
    return (y_p, y_s, sh_p, wkv_p, pool_p, sh_s, wkv_s, pool_s)
```

```python
import functools
import math

import jax
import jax.numpy as jnp
from jax import lax
from jax.experimental import pallas as pl
from jax.experimental.pallas import tpu as pltpu

F32 = jnp.float32
BF16 = jnp.bfloat16

HEAD_DIM = 64
POOL_WINDOWS = (2, 4, 8, 16)
POOL_HALO = 16
NORM_EPS = 1e-6
GN_EPS = 64e-5
PAST_LEN = 16384

VMEM_LIMIT_BYTES = 56 * 1024 * 1024
ROW_TILE = 256
PROJ_SLAB = 256
OUT_ROW_TILE = 256
POOL_ROW_TILE = 512
POOL_SUBTILES = 2
POOL_SHORT_ROW_TILE = 256
HEAD_SUM_SLAB = 256
WKV_ROW_TILE = 128
WKV_SEQS_PER_STEP = 2
WKV_SHORT_SEQS_PER_STEP = 4
WKV_CHUNK = 32
WKV_HEADS_PER_GROUP = 4


def _bdot(a, b):
    return jnp.dot(a.astype(BF16), b.astype(BF16), preferred_element_type=F32)


def _bdot_nt(a, b):
    return lax.dot_general(a.astype(BF16), b.astype(BF16), (((1,), (1,)), ((), ())),
                           preferred_element_type=F32)


def _bdot_tn(a, b):
    return lax.dot_general(a.astype(BF16), b.astype(BF16), (((0,), (0,)), ((), ())),
                           preferred_element_type=F32)


def _split2(x):
    hi = x.astype(BF16)
    lo = (x - hi.astype(F32)).astype(BF16)
    return hi, lo


def _head_sum(x, ones_ref):
    sel = ones_ref[...]
    b = sel.shape[0]
    xb = x.astype(BF16)
    return jnp.concatenate([jnp.dot(xb[:, i:i + b], sel, preferred_element_type=F32)
                            for i in range(0, x.shape[1], b)], axis=1)


def _sum(xs):
    return functools.reduce(lambda a, b: a + b, xs)


def _rms_norm(x, g):
    return x * lax.rsqrt(jnp.mean(x * x, axis=-1, keepdims=True) + NORM_EPS) * g


def _sigmoid(x):
    return 1.0 / (1.0 + jnp.exp(-x))


def _const_spec(shape):
    nd = len(shape)
    return pl.BlockSpec(shape, lambda *_: (0,) * nd, pipeline_mode=pl.Buffered(1))


def _params(sem):
    return pltpu.CompilerParams(dimension_semantics=sem, vmem_limit_bytes=VMEM_LIMIT_BYTES)


def _rwkv_proj_kernel(*refs, seq_len, tile, long_mode, has_vres):
    it = iter(refs)
    x_ref = next(it)
    halo_ref = next(it)
    norm_ref, mu_ref, wr_ref, wk_ref, wv_ref, wz_ref = (next(it) for _ in range(6))
    w0_ref, w1_ref, w2_ref, a0_ref, a1_ref, a2_ref = (next(it) for _ in range(6))
    kk_ref, ka_ref, rk_ref, ones_ref = (next(it) for _ in range(4))
    if has_vres:
        v0_ref, v1_ref, v2_ref, vfirst_ref = (next(it) for _ in range(4))
    r_o, lw_o, kh_o, v_o, kkn_o, ba_o, bonus_o, sz_o, xn_o = (next(it) for _ in range(9))
    carry_ref = next(it) if long_mode else None

    x = x_ref[...]
    xn = _rms_norm(x, norm_ref[...])
    rolled = pltpu.roll(xn, 1, 0)
    row = lax.broadcasted_iota(jnp.int32, xn.shape, 0)
    if long_mode:
        t = pl.program_id(1)

        @pl.when(t == 0)
        def _():
            carry_ref[...] = jnp.broadcast_to(halo_ref[0], carry_ref.shape)

        x_prev = jnp.where(row == 0, carry_ref[7:8, :], rolled)
        carry_ref[...] = xn[tile - 8:tile, :]
        xn_o[0] = xn[tile - 1:tile, :]
    else:
        x_prev = jnp.where((row & (seq_len - 1)) == 0, halo_ref[...], rolled)
        xn_o[...] = xn
    dx = x_prev - xn
    mu = mu_ref[...]
    xr, xw, xk, xv, xa, xg = ((xn + dx * mu[m:m + 1, :]).astype(BF16) for m in range(6))
    lora_w = jnp.tanh(_bdot(xw, w1_ref[...]))
    lora_a = _bdot(xa, a1_ref[...])
    if has_vres:
        lora_v = _bdot(xv, v1_ref[...])

    c = wr_ref.shape[1]
    slab = min(c, PROJ_SLAB)

    def slab_matmuls(j):
        sl = slice(j, j + slab)
        out = [_bdot(xr, wr_ref[:, sl]), _bdot(xk, wk_ref[:, sl]), _bdot(xv, wv_ref[:, sl]),
               _bdot(xg, wz_ref[:, sl]), _bdot(lora_w, w2_ref[:, sl]), _bdot(lora_a, a2_ref[:, sl])]
        if has_vres:
            out.append(_bdot(lora_v, v2_ref[:, sl]))
        return out

    starts = list(range(0, c, slab))
    ahead = slab_matmuls(starts[0])
    for idx, j in enumerate(starts):
        sl = slice(j, j + slab)
        r, k, v, z, w_lora, a_lora = ahead[:6]
        v_lora = ahead[6] if has_vres else None
        if idx + 1 < len(starts):
            ahead = slab_matmuls(starts[idx + 1])

        neg = -(w0_ref[:, sl] + w_lora)
        softplus = jnp.maximum(neg, 0.0) + jnp.log(1.0 + jnp.exp(-jnp.abs(neg)))
        lw_o[:, sl] = -jnp.exp(-softplus - 0.5)

        a = _sigmoid(a0_ref[:, sl] + a_lora)
        if has_vres:
            gate = _sigmoid(v0_ref[:, sl] + v_lora)
            v = v + (vfirst_ref[:, sl].astype(F32) - v) * gate

        kk = k * kk_ref[:, sl]
        nrm = jnp.maximum(jnp.sqrt(_head_sum(kk * kk, ones_ref)), 1e-12)
        kk = kk / nrm
        kh = k * (1.0 + (a - 1.0) * ka_ref[:, sl])
        bonus_o[:, sl] = (_head_sum(r * kh * rk_ref[:, sl], ones_ref) * v).astype(bonus_o.dtype)
        r_o[:, sl] = r.astype(r_o.dtype)
        kh_o[:, sl] = kh.astype(kh_o.dtype)
        v_o[:, sl] = v.astype(v_o.dtype)
        kkn_o[:, sl] = kk.astype(kkn_o.dtype)
        ba_o[:, sl] = (kk * a).astype(ba_o.dtype)
        sz_o[:, sl] = (z * _sigmoid(z)).astype(sz_o.dtype)


def _rwkv_proj(x2d, halo, seq_len, p, vres, vfirst):
    rows, d = x2d.shape
    c = p['w_r'].shape[1]
    nb = rows // seq_len
    long_mode = seq_len >= ROW_TILE
    tile = ROW_TILE if long_mode else min(ROW_TILE, rows)
    has_vres = vres is not None
    if long_mode:
        tps = seq_len // tile
        grid = (nb, tps)
        row_map = lambda b, t: (b * tps + t, 0)
        halo_spec = pl.BlockSpec((1, 1, d), lambda b, t: (b, 0, 0))
        xn_shape = jax.ShapeDtypeStruct((nb, 1, d), F32)
        xn_spec = pl.BlockSpec((1, 1, d), lambda b, t: (b, 0, 0))
        scratch = [pltpu.VMEM((8, d), F32)]
        sem = ("arbitrary", "arbitrary")
    else:
        grid = (rows // tile,)
        row_map = lambda i: (i, 0)
        halo_spec = pl.BlockSpec((tile, d), row_map)
        xn_shape = jax.ShapeDtypeStruct((rows, d), F32)
        xn_spec = pl.BlockSpec((tile, d), row_map)
        scratch = []
        sem = ("arbitrary",)
    row_d = pl.BlockSpec((tile, d), row_map)
    row_c = pl.BlockSpec((tile, c), row_map)
    consts = [p['norm'], p['mu'], p['w_r'], p['w_k'], p['w_v'], p['w_z'], p['w0'], p['w1'], p['w2'],
              p['a0'], p['a1'], p['a2'], p['k_k'], p['k_a'], p['r_k'], p['ones']]
    args = [x2d, halo] + consts
    in_specs = [row_d, halo_spec] + [_const_spec(a.shape) for a in consts]
    if has_vres:
        args += list(vres) + [vfirst]
        in_specs += [_const_spec(a.shape) for a in vres] + [row_c]
    out_dtypes = [BF16, F32, BF16, BF16, BF16, BF16, BF16, BF16]
    out_shape = [jax.ShapeDtypeStruct((rows, c), dt) for dt in out_dtypes] + [xn_shape]
    out_specs = [row_c] * 8 + [xn_spec]
    return pl.pallas_call(
        functools.partial(_rwkv_proj_kernel, seq_len=seq_len, tile=tile, long_mode=long_mode,
                          has_vres=has_vres),
        grid=grid, in_specs=in_specs, out_specs=out_specs, out_shape=out_shape,
        scratch_shapes=scratch, compiler_params=_params(sem), name="rwkv_proj",
    )(*args)


def _wkv_masks(chunk, hpg, op_dtype):
    w = HEAD_DIM * hpg
    n = chunk * hpg
    sh = int(math.log2(chunk))
    lane = lax.broadcasted_iota(jnp.int32, (1, w), 1)
    op_masks = [((lane >> 6) == e).astype(op_dtype) for e in range(hpg)]
    t2 = lax.broadcasted_iota(jnp.int32, (chunk, n), 0)
    s2 = lax.broadcasted_iota(jnp.int32, (chunk, n), 1) & (chunk - 1)
    strict = (s2 < t2).astype(F32)
    incl = (s2 <= t2).astype(F32)
    eye_row = (s2 == t2).astype(F32)
    rb = lax.broadcasted_iota(jnp.int32, (n, n), 0)
    cb = lax.broadcasted_iota(jnp.int32, (n, n), 1)
    bmask = ((rb >> sh) == (cb >> sh)).astype(F32)
    return op_masks, strict, incl, eye_row, bmask


def _load_state(s_ref, seq, gi, hpg):
    return jnp.concatenate([s_ref[seq, gi * hpg + e] for e in range(hpg)], axis=1)


def _store_state(s_ref, seq, gi, hpg, s_cat):
    for e in range(hpg):
        s_ref[seq, gi * hpg + e] = s_cat[:, e * HEAD_DIM:(e + 1) * HEAD_DIM]


def _chunk_tri(n, chunk):
    sh = int(math.log2(chunk))
    ti = lax.broadcasted_iota(jnp.int32, (n, n), 0)
    si = lax.broadcasted_iota(jnp.int32, (n, n), 1)
    return (((ti >> sh) == (si >> sh)) & (ti >= si)).astype(BF16)


def _wkv_operands(lw, r, k, v, kk, ba, tri, chunk, dt):
    nr, c_dim = lw.shape
    c = _sum([jnp.dot(tri, q, preferred_element_type=F32) for q in _split2(lw)])
    c_last = jnp.concatenate(
        [jnp.broadcast_to(c[q + chunk - 1:q + chunk, :], (chunk, c_dim)) for q in range(0, nr, chunk)], axis=0)
    scaled = lambda x, factor: x * factor.astype(dt)
    e_inv = jnp.exp(-c)
    e_hat = jnp.exp(c_last - c)
    return (scaled(kk, -jnp.exp(c - lw)), scaled(r, jnp.exp(c)), scaled(ba, e_inv), scaled(k, e_inv),
            scaled(ba, e_hat), scaled(k, e_hat), v, c_last)


def _wkv_chunk_step(states, row_starts, ops, clast_ref, masks, chunk, hpg, groups):
    at_ref, rt_ref, bt_ref, kt_ref, bh_ref, kh_ref, vb_ref = ops
    op_masks, strict, incl, eye_row, bmask = masks
    low2 = jnp.concatenate([strict, incl], axis=0)
    w = HEAD_DIM * hpg
    chains = [(s, g) for s in range(len(row_starts)) for g in range(groups)]

    def tile(ref, s, g):
        return ref[pl.ds(row_starts[s], chunk), g * w:(g + 1) * w]

    def stack(m):
        return jnp.concatenate([m.astype(lm.dtype) * lm for lm in op_masks], axis=0)

    def block_diag(m):
        return jnp.concatenate([m] * hpg, axis=0) * bmask

    lhs2 = [jnp.concatenate([tile(at_ref, s, g), tile(rt_ref, s, g)], axis=0) for s, g in chains]
    v_t = [tile(vb_ref, s, g) for s, g in chains]
    v_st = [stack(x) for x in v_t]
    n = hpg * chunk
    sc = [_bdot_nt(l, jnp.concatenate([stack(tile(bt_ref, s, g)), stack(tile(kt_ref, s, g))], axis=0))
          for l, (s, g) in zip(lhs2, chains)]
    sr = [_bdot_nt(l, stack(states[s][g])) for l, (s, g) in zip(lhs2, chains)]

    x_ab = [m[:chunk, :n] * strict for m in sc]
    x_rb = [m[chunk:, :n] * incl for m in sc]

    pw = [_bdot(a, block_diag(a)) for a in x_ab]
    srk = [m + _bdot(q[:, n:] * low2, vs) for m, q, vs in zip(sr, sc, v_st)]
    xa = [m[:chunk] for m in srk]
    y_k = [m[chunk:] for m in srk]
    t_inv = [eye_row + a for a in x_ab]
    for _ in range(int(math.log2(chunk)) - 2):
        t_new = [t + _bdot(p, block_diag(t)) for t, p in zip(t_inv, pw)]
        pw = [_bdot(p, block_diag(p)) for p in pw]
        t_inv = t_new
    t_inv = [t + _bdot(p, block_diag(t)) for t, p in zip(t_inv, pw)]

    u = [_bdot(t, stack(x)) for t, x in zip(t_inv, xa)]
    y = [yk + _bdot(x, stack(uu)) for yk, x, uu in zip(y_k, x_rb, u)]
    upd = []
    for uu, vv, (s, g) in zip(u, v_t, chains):
        uv = jnp.concatenate([uu, vv.astype(F32)], axis=0)
        uv_heads = jnp.concatenate([uv[:, e * HEAD_DIM:(e + 1) * HEAD_DIM] for e in range(hpg)], axis=0)
        bk = stack(jnp.concatenate([tile(bh_ref, s, g), tile(kh_ref, s, g)], axis=0))
        upd.append(_bdot_tn(uv_heads, bk))

    new_states = [[None] * groups for _ in row_starts]
    ys = [[None] * groups for _ in row_starts]
    for (s, g), up, yy in zip(chains, upd, y):
        decay = jnp.exp(clast_ref[pl.ds(row_starts[s], 1), g * w:(g + 1) * w])
        new_states[s][g] = states[s][g] * decay + up
        ys[s][g] = yy
    return new_states, ys


def _wkv_kernel(r_ref, lw_ref, k_ref, v_ref, kk_ref, ba_ref, s_in_ref, *rest,
                chunk, hpg, groups, nseq, rows, long_mode, own_layer):
    if own_layer is None:
        rest = rest[1:]
    y_ref, s_out_ref, *scratch = rest
    if own_layer is not None:
        s_all_ref = s_out_ref
        s_out_ref = s_all_ref.at[own_layer]

        def clear_other_layers():
            for layer in range(s_all_ref.shape[0]):
                if layer != own_layer:
                    s_all_ref[layer] = jnp.zeros(s_all_ref.shape[1:], F32)
    ops = tuple(scratch[:7])
    clast_ref = scratch[7]
    w = HEAD_DIM * hpg
    dt = ops[0].dtype
    masks = _wkv_masks(chunk, hpg, dt)
    srcs = (lw_ref, r_ref, k_ref, v_ref, kk_ref, ba_ref)

    if long_mode:
        state_ref = scratch[8]
        t = pl.program_id(1)

        @pl.when(t == 0)
        def _():
            for s in range(nseq):
                for g in range(groups):
                    state_ref[s, g] = _load_state(s_in_ref, s, g, hpg)

        tri = _chunk_tri(chunk, chunk)

        def prepare(cj):
            src = pl.ds(cj * chunk, chunk)
            for s in range(nseq):
                dst = pl.ds(s * rows + cj * chunk, chunk)
                vals = _wkv_operands(*[ref[s, src, :] for ref in srcs], tri, chunk, dt)
                for ref, val in zip(ops + (clast_ref,), vals):
                    ref[dst, :] = val

        def step(ci):
            starts = [s * rows + ci * chunk for s in range(nseq)]
            states = [[state_ref[s, g] for g in range(groups)] for s in range(nseq)]
            new_states, ys = _wkv_chunk_step(states, starts, ops, clast_ref, masks, chunk, hpg, groups)
            for s in range(nseq):
                for g in range(groups):
                    state_ref[s, g] = new_states[s][g]
                    y_ref[s, pl.ds(ci * chunk, chunk), g * w:(g + 1) * w] = ys[s][g].astype(y_ref.dtype)

        n_chunks = rows // chunk
        prepare(0)
        for ci in range(n_chunks):
            if ci + 1 < n_chunks:
                prepare(ci + 1)
            step(ci)

        @pl.when(t == pl.num_programs(1) - 1)
        def _():
            for s in range(nseq):
                for g in range(groups):
                    _store_state(s_out_ref, s, g, hpg, state_ref[s, g])
            if own_layer is not None:
                clear_other_layers()
    else:
        nr, c_dim = ops[0].shape
        vals = _wkv_operands(*[ref[...].astype(F32).reshape(nr, c_dim) for ref in srcs],
                             _chunk_tri(nr, chunk), chunk, dt)
        for ref, val in zip(ops + (clast_ref,), vals):
            ref[...] = val
        starts = [s * rows for s in range(nseq)]
        states = [[_load_state(s_in_ref, s, g, hpg) for g in range(groups)] for s in range(nseq)]
        new_states, ys = _wkv_chunk_step(states, starts, ops, clast_ref, masks, chunk, hpg, groups)
        for s in range(nseq):
            for g in range(groups):
                _store_state(s_out_ref, s, g, hpg, new_states[s][g])
                y_ref[s, :, g * w:(g + 1) * w] = ys[s][g].astype(y_ref.dtype)
        if own_layer is not None:
            clear_other_layers()


def _wkv(r, lw, kh, v, kk, ba, s_in, s_all, layer, n_layers, seq_len):
    total, c = r.shape
    nb = total // seq_len
    hpg = WKV_HEADS_PER_GROUP
    groups = c // (HEAD_DIM * hpg)
    n_heads = c // HEAD_DIM
    long_mode = seq_len > WKV_CHUNK
    first = s_all is None
    if long_mode:
        chunk, nseq, rows = WKV_CHUNK, WKV_SEQS_PER_STEP, min(WKV_ROW_TILE, seq_len)
        tps = seq_len // rows
        grid = (nb // nseq, tps)
        row_spec = pl.BlockSpec((nseq, None, rows, c), lambda b, t: (b, t, 0, 0))
        s_spec = pl.BlockSpec((nseq, n_heads, HEAD_DIM, HEAD_DIM), lambda b, t: (b, 0, 0, 0))
        s_out_map = lambda b, t: (0 if first else layer, b, 0, 0, 0)
        y_dtype = BF16
        shape4 = (nb, tps, rows, c)
        op_dtype = BF16
        sem = ("arbitrary", "arbitrary")
    else:
        chunk, nseq, rows = seq_len, WKV_SHORT_SEQS_PER_STEP, seq_len
        grid = (nb // nseq,)
        row_spec = pl.BlockSpec((nseq, None, rows, c), lambda b: (b, 0, 0, 0))
        s_spec = pl.BlockSpec((nseq, n_heads, HEAD_DIM, HEAD_DIM), lambda b: (b, 0, 0, 0))
        s_out_map = lambda b: (0 if first else layer, b, 0, 0, 0)
        y_dtype = F32
        shape4 = (nb, 1, rows, c)
        op_dtype = F32
        sem = ("arbitrary",)
    scratch = [pltpu.VMEM((nseq * rows, c), op_dtype)] * 7 + [pltpu.VMEM((nseq * rows, c), F32)]
    if long_mode:
        scratch.append(pltpu.VMEM((nseq, groups, HEAD_DIM, HEAD_DIM * hpg), F32))
    ins = [a.reshape(shape4) for a in (r, lw, kh, v, kk, ba)] + [s_in]
    in_specs = [row_spec] * 6 + [s_spec]
    s_out_spec = pl.BlockSpec((n_layers if first else None, nseq, n_heads, HEAD_DIM, HEAD_DIM), s_out_map)
    if not first:
        ins.append(s_all)
        in_specs.append(pl.BlockSpec(memory_space=pl.ANY))
    y, s_out = pl.pallas_call(
        functools.partial(_wkv_kernel, chunk=chunk, hpg=hpg, groups=groups, nseq=nseq, rows=rows,
                          long_mode=long_mode, own_layer=layer if first else None),
        grid=grid, in_specs=in_specs, out_specs=[row_spec, s_out_spec],
        out_shape=[jax.ShapeDtypeStruct(shape4, y_dtype),
                   jax.ShapeDtypeStruct((n_layers,) + s_in.shape, F32)],
        input_output_aliases={} if first else {len(ins) - 1: 1},
        scratch_shapes=scratch, compiler_params=_params(sem), name="wkv7_chunked",
    )(*ins)
    return y.reshape(total, c), s_out


def _rwkv_out_kernel(y_ref, bonus_ref, sz_ref, x_ref, gng_ref, gnb_ref, wo_ref, ones_ref, o_ref):
    y = y_ref[...].astype(F32)
    inv_n = 1.0 / HEAD_DIM
    mean = _head_sum(y, ones_ref) * inv_n
    d = y - mean
    var = _head_sum(d * d, ones_ref) * inv_n
    yn = d * lax.rsqrt(var + GN_EPS) * gng_ref[...] + gnb_ref[...]
    gated = (yn + bonus_ref[...].astype(F32)) * sz_ref[...].astype(F32)
    o_ref[...] = x_ref[...] + _bdot(gated, wo_ref[...])


def _rwkv_out(y, bonus, sz, x2d, p):
    rows, c = y.shape
    d = x2d.shape[1]
    tile = min(OUT_ROW_TILE, rows)
    row_c = pl.BlockSpec((tile, c), lambda i: (i, 0))
    row_d = pl.BlockSpec((tile, d), lambda i: (i, 0))
    consts = [p['gn_g'], p['gn_b'], p['w_o'], p['ones']]
    return pl.pallas_call(
        _rwkv_out_kernel, grid=(rows // tile,),
        in_specs=[row_c, row_c, row_c, row_d] + [_const_spec(a.shape) for a in consts],
        out_specs=row_d, out_shape=jax.ShapeDtypeStruct((rows, d), F32),
        compiler_params=_params(("arbitrary",)), name="rwkv_out",
    )(y, bonus, sz, x2d, *consts)


def _pool_kernel(*refs, seq_len, tile, long_mode, pos0, final):
    it = iter(refs)
    x_ref, buf_ref, norm_ref, win_ref, wgrp_ref, bgrp_ref, scale_ref, wo_ref = (next(it) for _ in range(8))
    fnorm_ref = next(it) if final else None
    o_ref, nbuf_ref = next(it), next(it)
    u_ref, ext_ref, p_ref = next(it), next(it), next(it)
    carry_ref = next(it) if long_mode else None

    c = u_ref.shape[1]
    pg = c // len(POOL_WINDOWS)
    n_sub = POOL_SUBTILES if long_mode else 1
    sub = tile // n_sub
    tl = sub if long_mode else seq_len
    nbuf_rows = nbuf_ref.shape[1]

    gates = []
    for h in range(n_sub):
        rows = slice(h * sub, (h + 1) * sub)
        xb = _rms_norm(x_ref[rows, :], norm_ref[...]).astype(BF16)
        u_ref[rows, :] = _bdot(xb, win_ref[:, :c])
        gates.append(_bdot(xb, win_ref[:, c:]))

    step = lax.broadcasted_iota(jnp.int32, (tl, 1), 0)

    def pool_one(base, pos):
        ext_ref[POOL_HALO:POOL_HALO + tl, :] = u_ref[pl.ds(base, tl), :]
        for g, win in enumerate(POOL_WINDOWS):
            lanes = slice(g * pg, (g + 1) * pg)
            s = ext_ref[:, lanes]
            dlt = 1
            while dlt < win:
                s = s + pltpu.roll(s, dlt, 0)
                dlt *= 2
            cnt = jnp.minimum(win, pos + 1).astype(F32)
            cur = ext_ref[POOL_HALO:POOL_HALO + tl, lanes]
            p_ref[pl.ds(base, tl), lanes] = s[POOL_HALO:POOL_HALO + tl, :] / cnt - cur

    def load_halo(seq):
        ext_ref[0:POOL_HALO - nbuf_rows, :] = jnp.zeros((POOL_HALO - nbuf_rows, c), F32)
        ext_ref[POOL_HALO - nbuf_rows:POOL_HALO, :] = buf_ref[seq]

    def finish(h):
        rows = slice(h * sub, (h + 1) * sub)
        p = p_ref[rows, :]
        mixed = jnp.concatenate(
            [_bdot(p[:, g * pg:(g + 1) * pg], wgrp_ref[g]) for g in range(len(POOL_WINDOWS))], axis=1)
        mixed = (mixed + bgrp_ref[...]) * scale_ref[...]
        z = gates[h]
        out = x_ref[rows, :] + _bdot(mixed * (z * _sigmoid(z)), wo_ref[...])
        if final:
            out = _rms_norm(out, fnorm_ref[...])
        o_ref[rows, :] = out

    if long_mode:
        t = pl.program_id(1)

        @pl.when(t == 0)
        def _():
            load_halo(0)

        @pl.when(t > 0)
        def _():
            ext_ref[0:POOL_HALO, :] = carry_ref[...]

        for h in range(n_sub):
            if h > 0:
                ext_ref[0:POOL_HALO, :] = ext_ref[tl:tl + POOL_HALO, :]
            pool_one(h * sub, pos0 + t * tile + h * sub + step)
            if h == n_sub - 1:
                carry_ref[...] = ext_ref[tl:tl + POOL_HALO, :]
                nbuf_ref[0] = ext_ref[tl + POOL_HALO - nbuf_rows:tl + POOL_HALO, :]
            finish(h)
    else:
        def body(s, carry):
            load_halo(s)
            pool_one(pl.multiple_of(s * seq_len, seq_len), pos0 + step)
            nbuf_ref[s] = ext_ref[tl + POOL_HALO - nbuf_rows:tl + POOL_HALO, :]
            return carry

        lax.fori_loop(0, tile // seq_len, body, 0)
        finish(0)


def _pool_layer(x2d, buf_all, layer, seq_len, pos0, p, final_norm):
    rows, d = x2d.shape
    c = p['w_o'].shape[0]
    nb = rows // seq_len
    long_mode = seq_len >= POOL_ROW_TILE
    final = final_norm is not None
    nbuf_rows = buf_all.shape[2]
    if long_mode:
        tile = POOL_ROW_TILE
        tps = seq_len // tile
        grid = (nb, tps)
        row_spec = pl.BlockSpec((tile, d), lambda b, t: (b * tps + t, 0))
        buf_spec = pl.BlockSpec((None, 1, nbuf_rows, c), lambda b, t: (layer, b, 0, 0))
        tl = tile // POOL_SUBTILES
        sem = ("arbitrary", "arbitrary")
    else:
        tile = min(POOL_SHORT_ROW_TILE, rows)
        grid = (rows // tile,)
        row_spec = pl.BlockSpec((tile, d), lambda i: (i, 0))
        buf_spec = pl.BlockSpec((None, tile // seq_len, nbuf_rows, c), lambda i: (layer, i, 0, 0))
        tl = seq_len
        sem = ("arbitrary",)
    consts = [p['norm'], p['w_in'], p['w_grp'], p['b_grp'], p['scale'], p['w_o']]
    if final:
        consts.append(final_norm)
    scratch = [pltpu.VMEM((tile, c), F32), pltpu.VMEM((POOL_HALO + tl, c), F32), pltpu.VMEM((tile, c), F32)]
    if long_mode:
        scratch.append(pltpu.VMEM((POOL_HALO, c), F32))
    return pl.pallas_call(
        functools.partial(_pool_kernel, seq_len=seq_len, tile=tile, long_mode=long_mode, pos0=pos0,
                          final=final),
        grid=grid, in_specs=[row_spec, buf_spec] + [_const_spec(a.shape) for a in consts],
        out_specs=[row_spec, buf_spec],
        out_shape=[jax.ShapeDtypeStruct((rows, d), F32), jax.ShapeDtypeStruct(buf_all.shape, F32)],
        input_output_aliases={1: 1},
        scratch_shapes=scratch, compiler_params=_params(sem), name="pool_mixer",
    )(x2d, buf_all, *consts)


def _trunk(x, shift0, wkv0, buf0, pos0, rw, pw, vres, final_norm):
    nb, seq_len, d = x.shape
    x2d = x.reshape(nb * seq_len, d)
    new_shift = []
    wkv_new = None
    v_first = None
    depth = len(rw) + len(pw)
    for i in range(depth):
        j = i // 2
        if i % 2 == 0:
            p = rw[j]
            if seq_len >= ROW_TILE:
                halo = shift0[j][:, None, :]
            else:
                halo = jnp.repeat(shift0[j], seq_len, axis=0)
            r, lw, kh, v, kk, ba, bonus, sz, xn_last = _rwkv_proj(
                x2d, halo, seq_len, p, None if j == 0 else vres[j - 1], v_first)
            if j == 0:
                v_first = v
            y, wkv_new = _wkv(r, lw, kh, v, kk, ba, wkv0[j], wkv_new, j, len(rw), seq_len)
            x2d = _rwkv_out(y, bonus, sz, x2d, p)
            new_shift.append(xn_last.reshape(nb, -1, d)[:, -1, :])
        else:
            p = pw[j]
            x2d, buf0 = _pool_layer(x2d, buf0, j, seq_len, pos0, p,
                                    final_norm if i == depth - 1 else None)
    return x2d.reshape(nb, seq_len, d), jnp.stack(new_shift), wkv_new, buf0


def kernel(x_prompt, x_sample, state_shift, state_wkv, state_pool, rwkv_norm, rwkv_mu, rwkv_w_r, rwkv_w_k, rwkv_w_v, rwkv_w_z, rwkv_w0, rwkv_w1, rwkv_w2, rwkv_a0, rwkv_a1, rwkv_a2, rwkv_v0, rwkv_v1, rwkv_v2, rwkv_k_k, rwkv_k_a, rwkv_r_k, rwkv_gn_g, rwkv_gn_b, rwkv_w_o, pool_norm, pool_w_in, pool_w_grp, pool_b_grp, pool_scale, pool_w_o, final_norm):
    n_rwkv, d, c = rwkv_w_r.shape
    n_pool = pool_w_in.shape[0]
    n_heads = c // HEAD_DIM
    bf = lambda a: a.astype(BF16)
    row = lambda a: a.reshape(1, -1)
    slab = min(c, HEAD_SUM_SLAB)
    head_of_lane = jnp.arange(slab, dtype=jnp.int32) // HEAD_DIM
    ones = (head_of_lane[:, None] == head_of_lane[None, :]).astype(BF16)
    rw = [dict(norm=row(rwkv_norm[j]), mu=rwkv_mu[j], w_r=bf(rwkv_w_r[j]), w_k=bf(rwkv_w_k[j]),
               w_v=bf(rwkv_w_v[j]), w_z=bf(rwkv_w_z[j]), w0=row(rwkv_w0[j]), w1=bf(rwkv_w1[j]),
               w2=bf(rwkv_w2[j]), a0=row(rwkv_a0[j]), a1=bf(rwkv_a1[j]), a2=bf(rwkv_a2[j]),
               k_k=row(rwkv_k_k[j]), k_a=row(rwkv_k_a[j]), r_k=row(rwkv_r_k[j]),
               gn_g=row(rwkv_gn_g[j]), gn_b=row(rwkv_gn_b[j]), w_o=bf(rwkv_w_o[j]), ones=ones)
          for j in range(n_rwkv)]
    vres = [(row(rwkv_v0[j]), bf(rwkv_v1[j]), bf(rwkv_v2[j])) for j in range(rwkv_v0.shape[0])]
    pw = [dict(norm=row(pool_norm[j]), w_in=bf(pool_w_in[j]), w_grp=bf(pool_w_grp[j]),
               b_grp=row(pool_b_grp[j]), scale=row(pool_scale[j]), w_o=bf(pool_w_o[j]))
          for j in range(n_pool)]
    fn = row(final_norm)

    nb = x_prompt.shape[0]
    shift0 = jnp.zeros((n_rwkv, nb, d), state_shift.dtype)
    wkv0 = jnp.zeros((n_rwkv, nb, n_heads, HEAD_DIM, HEAD_DIM), state_wkv.dtype)
    buf0 = jnp.zeros((n_pool, nb, state_pool.shape[2], c), state_pool.dtype)
    y_p, sh_p, wkv_p, pool_p = _trunk(x_prompt, shift0, wkv0, buf0, 0, rw, pw, vres, fn)
    y_s, sh_s, wkv_s, pool_s = _trunk(x_sample, state_shift, state_wkv, state_pool, PAST_LEN, rw, pw,
                                      vres, fn)
    return (y_p, y_s, sh_p, wkv_p, pool_p, sh_s, wkv_s, pool_s)
```

```python
import functools
import math

import jax
import jax.numpy as jnp
from jax import lax
from jax.experimental import pallas as pl
from jax.experimental.pallas import tpu as pltpu

F32 = jnp.float32
BF16 = jnp.bfloat16

HEAD_DIM = 64
POOL_WINDOWS = (2, 4, 8, 16)
POOL_HALO = 16
NORM_EPS = 1e-6
GN_EPS = 64e-5
PAST_LEN = 16384

VMEM_LIMIT_BYTES = 56 * 1024 * 1024
ROW_TILE = 256
PROJ_SLAB = 256
OUT_ROW_TILE = 512
POOL_ROW_TILE = 512
POOL_SUBTILES = 2
POOL_SHORT_ROW_TILE = 256
HEAD_SUM_SLAB = 256
WKV_ROW_TILE = 128
WKV_SEQS_PER_STEP = 2
WKV_SHORT_SEQS_PER_STEP = 4
WKV_CHUNK = 32
WKV_HEADS_PER_GROUP = 4


def _bdot(a, b):
    return jnp.dot(a.astype(BF16), b.astype(BF16), preferred_element_type=F32)


def _bdot_nt(a, b):
    return lax.dot_general(a.astype(BF16), b.astype(BF16), (((1,), (1,)), ((), ())),
                           preferred_element_type=F32)


def _bdot_tn(a, b):
    return lax.dot_general(a.astype(BF16), b.astype(BF16), (((0,), (0,)), ((), ())),
                           preferred_element_type=F32)


def _split2(x):
    hi = x.astype(BF16)
    lo = (x - hi.astype(F32)).astype(BF16)
    return hi, lo


def _head_sum(x, ones_ref):
    sel = ones_ref[...]
    b = sel.shape[0]
    xb = x.astype(BF16)
    return jnp.concatenate([jnp.dot(xb[:, i:i + b], sel, preferred_element_type=F32)
                            for i in range(0, x.shape[1], b)], axis=1)


def _sum(xs):
    return functools.reduce(lambda a, b: a + b, xs)


def _rms_norm(x, g):
    return x * lax.rsqrt(jnp.mean(x * x, axis=-1, keepdims=True) + NORM_EPS) * g


def _sigmoid(x):
    return 1.0 / (1.0 + jnp.exp(-x))


def _const_spec(shape):
    nd = len(shape)
    return pl.BlockSpec(shape, lambda *_: (0,) * nd, pipeline_mode=pl.Buffered(1))


def _params(sem):
    return pltpu.CompilerParams(dimension_semantics=sem, vmem_limit_bytes=VMEM_LIMIT_BYTES)


def _rwkv_proj_kernel(*refs, seq_len, tile, long_mode, has_vres):
    it = iter(refs)
    x_ref = next(it)
    halo_ref = next(it)
    norm_ref, mu_ref, wr_ref, wk_ref, wv_ref, wz_ref = (next(it) for _ in range(6))
    w0_ref, w1_ref, w2_ref, a0_ref, a1_ref, a2_ref = (next(it) for _ in range(6))
    kk_ref, ka_ref, rk_ref, ones_ref = (next(it) for _ in range(4))
    if has_vres:
        v0_ref, v1_ref, v2_ref, vfirst_ref = (next(it) for _ in range(4))
    r_o, lw_o, kh_o, v_o, kkn_o, ba_o, bonus_o, sz_o, xn_o = (next(it) for _ in range(9))
    carry_ref = next(it) if long_mode else None

    x = x_ref[...]
    xn = _rms_norm(x, norm_ref[...])
    rolled = pltpu.roll(xn, 1, 0)
    row = lax.broadcasted_iota(jnp.int32, xn.shape, 0)
    if long_mode:
        t = pl.program_id(1)

        @pl.when(t == 0)
        def _():
            carry_ref[...] = jnp.broadcast_to(halo_ref[0], carry_ref.shape)

        x_prev = jnp.where(row == 0, carry_ref[7:8, :], rolled)
        carry_ref[...] = xn[tile - 8:tile, :]
        xn_o[0] = xn[tile - 1:tile, :]
    else:
        x_prev = jnp.where((row & (seq_len - 1)) == 0, halo_ref[...], rolled)
        xn_o[...] = xn
    dx = x_prev - xn
    mu = mu_ref[...]
    xr, xw, xk, xv, xa, xg = ((xn + dx * mu[m:m + 1, :]).astype(BF16) for m in range(6))
    lora_w = jnp.tanh(_bdot(xw, w1_ref[...]))
    lora_a = _bdot(xa, a1_ref[...])
    if has_vres:
        lora_v = _bdot(xv, v1_ref[...])

    c = wr_ref.shape[1]
    slab = min(c, PROJ_SLAB)

    def slab_matmuls(j):
        sl = slice(j, j + slab)
        out = [_bdot(xr, wr_ref[:, sl]), _bdot(xk, wk_ref[:, sl]), _bdot(xv, wv_ref[:, sl]),
               _bdot(xg, wz_ref[:, sl]), _bdot(lora_w, w2_ref[:, sl]), _bdot(lora_a, a2_ref[:, sl])]
        if has_vres:
            out.append(_bdot(lora_v, v2_ref[:, sl]))
        return out

    starts = list(range(0, c, slab))
    ahead = slab_matmuls(starts[0])
    for idx, j in enumerate(starts):
        sl = slice(j, j + slab)
        r, k, v, z, w_lora, a_lora = ahead[:6]
        v_lora = ahead[6] if has_vres else None
        if idx + 1 < len(starts):
            ahead = slab_matmuls(starts[idx + 1])

        neg = -(w0_ref[:, sl] + w_lora)
        softplus = jnp.maximum(neg, 0.0) + jnp.log(1.0 + jnp.exp(-jnp.abs(neg)))
        lw_o[:, sl] = -jnp.exp(-softplus - 0.5)

        a = _sigmoid(a0_ref[:, sl] + a_lora)
        if has_vres:
            gate = _sigmoid(v0_ref[:, sl] + v_lora)
            v = v + (vfirst_ref[:, sl].astype(F32) - v) * gate

        kk = k * kk_ref[:, sl]
        nrm = jnp.maximum(jnp.sqrt(_head_sum(kk * kk, ones_ref)), 1e-12)
        kk = kk / nrm
        kh = k * (1.0 + (a - 1.0) * ka_ref[:, sl])
        bonus_o[:, sl] = (_head_sum(r * kh * rk_ref[:, sl], ones_ref) * v).astype(bonus_o.dtype)
        r_o[:, sl] = r.astype(r_o.dtype)
        kh_o[:, sl] = kh.astype(kh_o.dtype)
        v_o[:, sl] = v.astype(v_o.dtype)
        kkn_o[:, sl] = kk.astype(kkn_o.dtype)
        ba_o[:, sl] = (kk * a).astype(ba_o.dtype)
        sz_o[:, sl] = (z * _sigmoid(z)).astype(sz_o.dtype)


def _rwkv_proj(x2d, halo, seq_len, p, vres, vfirst):
    rows, d = x2d.shape
    c = p['w_r'].shape[1]
    nb = rows // seq_len
    long_mode = seq_len >= ROW_TILE
    tile = ROW_TILE if long_mode else min(ROW_TILE, rows)
    has_vres = vres is not None
    if long_mode:
        tps = seq_len // tile
        grid = (nb, tps)
        row_map = lambda b, t: (b * tps + t, 0)
        halo_spec = pl.BlockSpec((1, 1, d), lambda b, t: (b, 0, 0))
        xn_shape = jax.ShapeDtypeStruct((nb, 1, d), F32)
        xn_spec = pl.BlockSpec((1, 1, d), lambda b, t: (b, 0, 0))
        scratch = [pltpu.VMEM((8, d), F32)]
        sem = ("arbitrary", "arbitrary")
    else:
        grid = (rows // tile,)
        row_map = lambda i: (i, 0)
        halo_spec = pl.BlockSpec((tile, d), row_map)
        xn_shape = jax.ShapeDtypeStruct((rows, d), F32)
        xn_spec = pl.BlockSpec((tile, d), row_map)
        scratch = []
        sem = ("arbitrary",)
    row_d = pl.BlockSpec((tile, d), row_map)
    row_c = pl.BlockSpec((tile, c), row_map)
    consts = [p['norm'], p['mu'], p['w_r'], p['w_k'], p['w_v'], p['w_z'], p['w0'], p['w1'], p['w2'],
              p['a0'], p['a1'], p['a2'], p['k_k'], p['k_a'], p['r_k'], p['ones']]
    args = [x2d, halo] + consts
    in_specs = [row_d, halo_spec] + [_const_spec(a.shape) for a in consts]
    if has_vres:
        args += list(vres) + [vfirst]
        in_specs += [_const_spec(a.shape) for a in vres] + [row_c]
    out_dtypes = [BF16, F32, BF16, BF16, BF16, BF16, BF16, BF16]
    out_shape = [jax.ShapeDtypeStruct((rows, c), dt) for dt in out_dtypes] + [xn_shape]
    out_specs = [row_c] * 8 + [xn_spec]
    return pl.pallas_call(
        functools.partial(_rwkv_proj_kernel, seq_len=seq_len, tile=tile, long_mode=long_mode,
                          has_vres=has_vres),
        grid=grid, in_specs=in_specs, out_specs=out_specs, out_shape=out_shape,
        scratch_shapes=scratch, compiler_params=_params(sem), name="rwkv_proj",
    )(*args)


def _wkv_masks(chunk, hpg, op_dtype):
    w = HEAD_DIM * hpg
    n = chunk * hpg
    sh = int(math.log2(chunk))
    lane = lax.broadcasted_iota(jnp.int32, (1, w), 1)
    op_masks = [((lane >> 6) == e).astype(op_dtype) for e in range(hpg)]
    t2 = lax.broadcasted_iota(jnp.int32, (chunk, n), 0)
    s2 = lax.broadcasted_iota(jnp.int32, (chunk, n), 1) & (chunk - 1)
    strict = (s2 < t2).astype(F32)
    incl = (s2 <= t2).astype(F32)
    eye_row = (s2 == t2).astype(F32)
    rb = lax.broadcasted_iota(jnp.int32, (n, n), 0)
    cb = lax.broadcasted_iota(jnp.int32, (n, n), 1)
    bmask = ((rb >> sh) == (cb >> sh)).astype(F32)
    return op_masks, strict, incl, eye_row, bmask


def _load_state(s_ref, seq, gi, hpg):
    return jnp.concatenate([s_ref[seq, gi * hpg + e] for e in range(hpg)], axis=1)


def _store_state(s_ref, seq, gi, hpg, s_cat):
    for e in range(hpg):
        s_ref[seq, gi * hpg + e] = s_cat[:, e * HEAD_DIM:(e + 1) * HEAD_DIM]


def _chunk_tri(n, chunk):
    sh = int(math.log2(chunk))
    ti = lax.broadcasted_iota(jnp.int32, (n, n), 0)
    si = lax.broadcasted_iota(jnp.int32, (n, n), 1)
    return (((ti >> sh) == (si >> sh)) & (ti >= si)).astype(BF16)


def _wkv_operands(lw, r, k, v, kk, ba, tri, chunk, dt):
    nr, c_dim = lw.shape
    c = _sum([jnp.dot(tri, q, preferred_element_type=F32) for q in _split2(lw)])
    c_last = jnp.concatenate(
        [jnp.broadcast_to(c[q + chunk - 1:q + chunk, :], (chunk, c_dim)) for q in range(0, nr, chunk)], axis=0)
    scaled = lambda x, factor: x * factor.astype(dt)
    e_inv = jnp.exp(-c)
    e_hat = jnp.exp(c_last - c)
    return (scaled(kk, -jnp.exp(c - lw)), scaled(r, jnp.exp(c)), scaled(ba, e_inv), scaled(k, e_inv),
            scaled(ba, e_hat), scaled(k, e_hat), v, c_last)


def _wkv_chunk_step(states, row_starts, ops, clast_ref, masks, chunk, hpg, groups):
    at_ref, rt_ref, bt_ref, kt_ref, bh_ref, kh_ref, vb_ref = ops
    op_masks, strict, incl, eye_row, bmask = masks
    low2 = jnp.concatenate([strict, incl], axis=0)
    w = HEAD_DIM * hpg
    chains = [(s, g) for s in range(len(row_starts)) for g in range(groups)]

    def tile(ref, s, g):
        return ref[pl.ds(row_starts[s], chunk), g * w:(g + 1) * w]

    def stack(m):
        return jnp.concatenate([m.astype(lm.dtype) * lm for lm in op_masks], axis=0)

    def block_diag(m):
        return jnp.concatenate([m] * hpg, axis=0) * bmask

    lhs2 = [jnp.concatenate([tile(at_ref, s, g), tile(rt_ref, s, g)], axis=0) for s, g in chains]
    v_t = [tile(vb_ref, s, g) for s, g in chains]
    v_st = [stack(x) for x in v_t]
    n = hpg * chunk
    sc = [_bdot_nt(l, jnp.concatenate([stack(tile(bt_ref, s, g)), stack(tile(kt_ref, s, g))], axis=0))
          for l, (s, g) in zip(lhs2, chains)]
    sr = [_bdot_nt(l, stack(states[s][g])) for l, (s, g) in zip(lhs2, chains)]

    x_ab = [m[:chunk, :n] * strict for m in sc]
    x_rb = [m[chunk:, :n] * incl for m in sc]

    pw = [_bdot(a, block_diag(a)) for a in x_ab]
    srk = [m + _bdot(q[:, n:] * low2, vs) for m, q, vs in zip(sr, sc, v_st)]
    xa = [m[:chunk] for m in srk]
    y_k = [m[chunk:] for m in srk]
    t_inv = [eye_row + a for a in x_ab]
    for _ in range(int(math.log2(chunk)) - 2):
        t_new = [t + _bdot(p, block_diag(t)) for t, p in zip(t_inv, pw)]
        pw = [_bdot(p, block_diag(p)) for p in pw]
        t_inv = t_new
    t_inv = [t + _bdot(p, block_diag(t)) for t, p in zip(t_inv, pw)]

    u = [_bdot(t, stack(x)) for t, x in zip(t_inv, xa)]
    y = [yk + _bdot(x, stack(uu)) for yk, x, uu in zip(y_k, x_rb, u)]
    upd = []
    for uu, vv, (s, g) in zip(u, v_t, chains):
        uv = jnp.concatenate([uu, vv.astype(F32)], axis=0)
        uv_heads = jnp.concatenate([uv[:, e * HEAD_DIM:(e + 1) * HEAD_DIM] for e in range(hpg)], axis=0)
        bk = stack(jnp.concatenate([tile(bh_ref, s, g), tile(kh_ref, s, g)], axis=0))
        upd.append(_bdot_tn(uv_heads, bk))

    new_states = [[None] * groups for _ in row_starts]
    ys = [[None] * groups for _ in row_starts]
    for (s, g), up, yy in zip(chains, upd, y):
        decay = jnp.exp(clast_ref[pl.ds(row_starts[s], 1), g * w:(g + 1) * w])
        new_states[s][g] = states[s][g] * decay + up
        ys[s][g] = yy
    return new_states, ys


def _wkv_kernel(r_ref, lw_ref, k_ref, v_ref, kk_ref, ba_ref, s_in_ref, *rest,
                chunk, hpg, groups, nseq, rows, long_mode, own_layer):
    if own_layer is None:
        rest = rest[1:]
    y_ref, s_out_ref, *scratch = rest
    if own_layer is not None:
        s_all_ref = s_out_ref
        s_out_ref = s_all_ref.at[own_layer]

        def clear_other_layers():
            for layer in range(s_all_ref.shape[0]):
                if layer != own_layer:
                    s_all_ref[layer] = jnp.zeros(s_all_ref.shape[1:], F32)
    ops = tuple(scratch[:7])
    clast_ref = scratch[7]
    w = HEAD_DIM * hpg
    dt = ops[0].dtype
    masks = _wkv_masks(chunk, hpg, dt)
    srcs = (lw_ref, r_ref, k_ref, v_ref, kk_ref, ba_ref)

    if long_mode:
        state_ref = scratch[8]
        t = pl.program_id(1)

        @pl.when(t == 0)
        def _():
            for s in range(nseq):
                for g in range(groups):
                    state_ref[s, g] = _load_state(s_in_ref, s, g, hpg)

        tri = _chunk_tri(chunk, chunk)

        def prepare(cj):
            src = pl.ds(cj * chunk, chunk)
            for s in range(nseq):
                dst = pl.ds(s * rows + cj * chunk, chunk)
                vals = _wkv_operands(*[ref[s, src, :] for ref in srcs], tri, chunk, dt)
                for ref, val in zip(ops + (clast_ref,), vals):
                    ref[dst, :] = val

        def step(ci):
            starts = [s * rows + ci * chunk for s in range(nseq)]
            states = [[state_ref[s, g] for g in range(groups)] for s in range(nseq)]
            new_states, ys = _wkv_chunk_step(states, starts, ops, clast_ref, masks, chunk, hpg, groups)
            for s in range(nseq):
                for g in range(groups):
                    state_ref[s, g] = new_states[s][g]
                    y_ref[s, pl.ds(ci * chunk, chunk), g * w:(g + 1) * w] = ys[s][g].astype(y_ref.dtype)

        n_chunks = rows // chunk
        prepare(0)
        for ci in range(n_chunks):
            if ci + 1 < n_chunks:
                prepare(ci + 1)
            step(ci)

        @pl.when(t == pl.num_programs(1) - 1)
        def _():
            for s in range(nseq):
                for g in range(groups):
                    _store_state(s_out_ref, s, g, hpg, state_ref[s, g])
            if own_layer is not None:
                clear_other_layers()
    else:
        nr, c_dim = ops[0].shape
        vals = _wkv_operands(*[ref[...].astype(F32).reshape(nr, c_dim) for ref in srcs],
                             _chunk_tri(nr, chunk), chunk, dt)
        for ref, val in zip(ops + (clast_ref,), vals):
            ref[...] = val
        starts = [s * rows for s in range(nseq)]
        states = [[_load_state(s_in_ref, s, g, hpg) for g in range(groups)] for s in range(nseq)]
        new_states, ys = _wkv_chunk_step(states, starts, ops, clast_ref, masks, chunk, hpg, groups)
        for s in range(nseq):
            for g in range(groups):
                _store_state(s_out_ref, s, g, hpg, new_states[s][g])
                y_ref[s, :, g * w:(g + 1) * w] = ys[s][g].astype(y_ref.dtype)
        if own_layer is not None:
            clear_other_layers()


def _wkv(r, lw, kh, v, kk, ba, s_in, s_all, layer, n_layers, seq_len):
    total, c = r.shape
    nb = total // seq_len
    hpg = WKV_HEADS_PER_GROUP
    groups = c // (HEAD_DIM * hpg)
    n_heads = c // HEAD_DIM
    long_mode = seq_len > WKV_CHUNK
    first = s_all is None
    if long_mode:
        chunk, nseq, rows = WKV_CHUNK, WKV_SEQS_PER_STEP, min(WKV_ROW_TILE, seq_len)
        tps = seq_len // rows
        grid = (nb // nseq, tps)
        row_spec = pl.BlockSpec((nseq, None, rows, c), lambda b, t: (b, t, 0, 0))
        s_spec = pl.BlockSpec((nseq, n_heads, HEAD_DIM, HEAD_DIM), lambda b, t: (b, 0, 0, 0))
        s_out_map = lambda b, t: (0 if first else layer, b, 0, 0, 0)
        y_dtype = BF16
        shape4 = (nb, tps, rows, c)
        op_dtype = BF16
        sem = ("arbitrary", "arbitrary")
    else:
        chunk, nseq, rows = seq_len, WKV_SHORT_SEQS_PER_STEP, seq_len
        grid = (nb // nseq,)
        row_spec = pl.BlockSpec((nseq, None, rows, c), lambda b: (b, 0, 0, 0))
        s_spec = pl.BlockSpec((nseq, n_heads, HEAD_DIM, HEAD_DIM), lambda b: (b, 0, 0, 0))
        s_out_map = lambda b: (0 if first else layer, b, 0, 0, 0)
        y_dtype = F32
        shape4 = (nb, 1, rows, c)
        op_dtype = F32
        sem = ("arbitrary",)
    scratch = [pltpu.VMEM((nseq * rows, c), op_dtype)] * 7 + [pltpu.VMEM((nseq * rows, c), F32)]
    if long_mode:
        scratch.append(pltpu.VMEM((nseq, groups, HEAD_DIM, HEAD_DIM * hpg), F32))
    ins = [a.reshape(shape4) for a in (r, lw, kh, v, kk, ba)] + [s_in]
    in_specs = [row_spec] * 6 + [s_spec]
    s_out_spec = pl.BlockSpec((n_layers if first else None, nseq, n_heads, HEAD_DIM, HEAD_DIM), s_out_map)
    if not first:
        ins.append(s_all)
        in_specs.append(pl.BlockSpec(memory_space=pl.ANY))
    y, s_out = pl.pallas_call(
        functools.partial(_wkv_kernel, chunk=chunk, hpg=hpg, groups=groups, nseq=nseq, rows=rows,
                          long_mode=long_mode, own_layer=layer if first else None),
        grid=grid, in_specs=in_specs, out_specs=[row_spec, s_out_spec],
        out_shape=[jax.ShapeDtypeStruct(shape4, y_dtype),
                   jax.ShapeDtypeStruct((n_layers,) + s_in.shape, F32)],
        input_output_aliases={} if first else {len(ins) - 1: 1},
        scratch_shapes=scratch, compiler_params=_params(sem), name="wkv7_chunked",
    )(*ins)
    return y.reshape(total, c), s_out


def _rwkv_out_kernel(y_ref, bonus_ref, sz_ref, x_ref, gng_ref, gnb_ref, wo_ref, ones_ref, o_ref):
    y = y_ref[...].astype(F32)
    inv_n = 1.0 / HEAD_DIM
    mean = _head_sum(y, ones_ref) * inv_n
    d = y - mean
    var = _head_sum(d * d, ones_ref) * inv_n
    yn = d * lax.rsqrt(var + GN_EPS) * gng_ref[...] + gnb_ref[...]
    gated = (yn + bonus_ref[...].astype(F32)) * sz_ref[...].astype(F32)
    o_ref[...] = x_ref[...] + _bdot(gated, wo_ref[...])


def _rwkv_out(y, bonus, sz, x2d, p):
    rows, c = y.shape
    d = x2d.shape[1]
    tile = min(OUT_ROW_TILE, rows)
    row_c = pl.BlockSpec((tile, c), lambda i: (i, 0))
    row_d = pl.BlockSpec((tile, d), lambda i: (i, 0))
    consts = [p['gn_g'], p['gn_b'], p['w_o'], p['ones']]
    return pl.pallas_call(
        _rwkv_out_kernel, grid=(rows // tile,),
        in_specs=[row_c, row_c, row_c, row_d] + [_const_spec(a.shape) for a in consts],
        out_specs=row_d, out_shape=jax.ShapeDtypeStruct((rows, d), F32),
        compiler_params=_params(("arbitrary",)), name="rwkv_out",
    )(y, bonus, sz, x2d, *consts)


def _pool_kernel(*refs, seq_len, tile, long_mode, pos0, final):
    it = iter(refs)
    x_ref, buf_ref, norm_ref, win_ref, wgrp_ref, bgrp_ref, scale_ref, wo_ref = (next(it) for _ in range(8))
    fnorm_ref = next(it) if final else None
    o_ref, nbuf_ref = next(it), next(it)
    u_ref, ext_ref, p_ref = next(it), next(it), next(it)
    carry_ref = next(it) if long_mode else None

    c = u_ref.shape[1]
    pg = c // len(POOL_WINDOWS)
    n_sub = POOL_SUBTILES if long_mode else 1
    sub = tile // n_sub
    tl = sub if long_mode else seq_len
    nbuf_rows = nbuf_ref.shape[1]

    gates = []
    for h in range(n_sub):
        rows = slice(h * sub, (h + 1) * sub)
        xb = _rms_norm(x_ref[rows, :], norm_ref[...]).astype(BF16)
        u_ref[rows, :] = _bdot(xb, win_ref[:, :c])
        gates.append(_bdot(xb, win_ref[:, c:]))

    step = lax.broadcasted_iota(jnp.int32, (tl, 1), 0)

    def pool_one(base, pos):
        ext_ref[POOL_HALO:POOL_HALO + tl, :] = u_ref[pl.ds(base, tl), :]
        for g, win in enumerate(POOL_WINDOWS):
            lanes = slice(g * pg, (g + 1) * pg)
            s = ext_ref[:, lanes]
            dlt = 1
            while dlt < win:
                s = s + pltpu.roll(s, dlt, 0)
                dlt *= 2
            cnt = jnp.minimum(win, pos + 1).astype(F32)
            cur = ext_ref[POOL_HALO:POOL_HALO + tl, lanes]
            p_ref[pl.ds(base, tl), lanes] = s[POOL_HALO:POOL_HALO + tl, :] / cnt - cur

    def load_halo(seq):
        ext_ref[0:POOL_HALO - nbuf_rows, :] = jnp.zeros((POOL_HALO - nbuf_rows, c), F32)
        ext_ref[POOL_HALO - nbuf_rows:POOL_HALO, :] = buf_ref[seq]

    def finish(h):
        rows = slice(h * sub, (h + 1) * sub)
        p = p_ref[rows, :]
        mixed = jnp.concatenate(
            [_bdot(p[:, g * pg:(g + 1) * pg], wgrp_ref[g]) for g in range(len(POOL_WINDOWS))], axis=1)
        mixed = (mixed + bgrp_ref[...]) * scale_ref[...]
        z = gates[h]
        out = x_ref[rows, :] + _bdot(mixed * (z * _sigmoid(z)), wo_ref[...])
        if final:
            out = _rms_norm(out, fnorm_ref[...])
        o_ref[rows, :] = out

    if long_mode:
        t = pl.program_id(1)

        @pl.when(t == 0)
        def _():
            load_halo(0)

        @pl.when(t > 0)
        def _():
            ext_ref[0:POOL_HALO, :] = carry_ref[...]

        for h in range(n_sub):
            if h > 0:
                ext_ref[0:POOL_HALO, :] = ext_ref[tl:tl + POOL_HALO, :]
            pool_one(h * sub, pos0 + t * tile + h * sub + step)
            if h == n_sub - 1:
                carry_ref[...] = ext_ref[tl:tl + POOL_HALO, :]
                nbuf_ref[0] = ext_ref[tl + POOL_HALO - nbuf_rows:tl + POOL_HALO, :]
            finish(h)
    else:
        def body(s, carry):
            load_halo(s)
            pool_one(pl.multiple_of(s * seq_len, seq_len), pos0 + step)
            nbuf_ref[s] = ext_ref[tl + POOL_HALO - nbuf_rows:tl + POOL_HALO, :]
            return carry

        lax.fori_loop(0, tile // seq_len, body, 0)
        finish(0)


def _pool_layer(x2d, buf_all, layer, seq_len, pos0, p, final_norm):
    rows, d = x2d.shape
    c = p['w_o'].shape[0]
    nb = rows // seq_len
    long_mode = seq_len >= POOL_ROW_TILE
    final = final_norm is not None
    nbuf_rows = buf_all.shape[2]
    if long_mode:
        tile = POOL_ROW_TILE
        tps = seq_len // tile
        grid = (nb, tps)
        row_spec = pl.BlockSpec((tile, d), lambda b, t: (b * tps + t, 0))
        buf_spec = pl.BlockSpec((None, 1, nbuf_rows, c), lambda b, t: (layer, b, 0, 0))
        tl = tile // POOL_SUBTILES
        sem = ("arbitrary", "arbitrary")
    else:
        tile = min(POOL_SHORT_ROW_TILE, rows)
        grid = (rows // tile,)
        row_spec = pl.BlockSpec((tile, d), lambda i: (i, 0))
        buf_spec = pl.BlockSpec((None, tile // seq_len, nbuf_rows, c), lambda i: (layer, i, 0, 0))
        tl = seq_len
        sem = ("arbitrary",)
    consts = [p['norm'], p['w_in'], p['w_grp'], p['b_grp'], p['scale'], p['w_o']]
    if final:
        consts.append(final_norm)
    scratch = [pltpu.VMEM((tile, c), F32), pltpu.VMEM((POOL_HALO + tl, c), F32), pltpu.VMEM((tile, c), F32)]
    if long_mode:
        scratch.append(pltpu.VMEM((POOL_HALO, c), F32))
    return pl.pallas_call(
        functools.partial(_pool_kernel, seq_len=seq_len, tile=tile, long_mode=long_mode, pos0=pos0,
                          final=final),
        grid=grid, in_specs=[row_spec, buf_spec] + [_const_spec(a.shape) for a in consts],
        out_specs=[row_spec, buf_spec],
        out_shape=[jax.ShapeDtypeStruct((rows, d), F32), jax.ShapeDtypeStruct(buf_all.shape, F32)],
        input_output_aliases={1: 1},
        scratch_shapes=scratch, compiler_params=_params(sem), name="pool_mixer",
    )(x2d, buf_all, *consts)


def _trunk(x, shift0, wkv0, buf0, pos0, rw, pw, vres, final_norm):
    nb, seq_len, d = x.shape
    x2d = x.reshape(nb * seq_len, d)
    new_shift = []
    wkv_new = None
    v_first = None
    depth = len(rw) + len(pw)
    for i in range(depth):
        j = i // 2
        if i % 2 == 0:
            p = rw[j]
            if seq_len >= ROW_TILE:
                halo = shift0[j][:, None, :]
            else:
                halo = jnp.repeat(shift0[j], seq_len, axis=0)
            r, lw, kh, v, kk, ba, bonus, sz, xn_last = _rwkv_proj(
                x2d, halo, seq_len, p, None if j == 0 else vres[j - 1], v_first)
            if j == 0:
                v_first = v
            y, wkv_new = _wkv(r, lw, kh, v, kk, ba, wkv0[j], wkv_new, j, len(rw), seq_len)
            x2d = _rwkv_out(y, bonus, sz, x2d, p)
            new_shift.append(xn_last.reshape(nb, -1, d)[:, -1, :])
        else:
            p = pw[j]
            x2d, buf0 = _pool_layer(x2d, buf0, j, seq_len, pos0, p,
                                    final_norm if i == depth - 1 else None)
    return x2d.reshape(nb, seq_len, d), jnp.stack(new_shift), wkv_new, buf0


def kernel(x_prompt, x_sample, state_shift, state_wkv, state_pool, rwkv_norm, rwkv_mu, rwkv_w_r, rwkv_w_k, rwkv_w_v, rwkv_w_z, rwkv_w0, rwkv_w1, rwkv_w2, rwkv_a0, rwkv_a1, rwkv_a2, rwkv_v0, rwkv_v1, rwkv_v2, rwkv_k_k, rwkv_k_a, rwkv_r_k, rwkv_gn_g, rwkv_gn_b, rwkv_w_o, pool_norm, pool_w_in, pool_w_grp, pool_b_grp, pool_scale, pool_w_o, final_norm):
    n_rwkv, d, c = rwkv_w_r.shape
    n_pool = pool_w_in.shape[0]
    n_heads = c // HEAD_DIM
    bf = lambda a: a.astype(BF16)
    row = lambda a: a.reshape(1, -1)
    slab = min(c, HEAD_SUM_SLAB)
    head_of_lane = jnp.arange(slab, dtype=jnp.int32) // HEAD_DIM
    ones = (head_of_lane[:, None] == head_of_lane[None, :]).astype(BF16)
    rw = [dict(norm=row(rwkv_norm[j]), mu=rwkv_mu[j], w_r=bf(rwkv_w_r[j]), w_k=bf(rwkv_w_k[j]),
               w_v=bf(rwkv_w_v[j]), w_z=bf(rwkv_w_z[j]), w0=row(rwkv_w0[j]), w1=bf(rwkv_w1[j]),
               w2=bf(rwkv_w2[j]), a0=row(rwkv_a0[j]), a1=bf(rwkv_a1[j]), a2=bf(rwkv_a2[j]),
               k_k=row(rwkv_k_k[j]), k_a=row(rwkv_k_a[j]), r_k=row(rwkv_r_k[j]),
               gn_g=row(rwkv_gn_g[j]), gn_b=row(rwkv_gn_b[j]), w_o=bf(rwkv_w_o[j]), ones=ones)
          for j in range(n_rwkv)]
    vres = [(row(rwkv_v0[j]), bf(rwkv_v1[j]), bf(rwkv_v2[j])) for j in range(rwkv_v0.shape[0])]
    pw = [dict(norm=row(pool_norm[j]), w_in=bf(pool_w_in[j]), w_grp=bf(pool_w_grp[j]),
               b_grp=row(pool_b_grp[j]), scale=row(pool_scale[j]), w_o=bf(pool_w_o[j]))
          for j in range(n_pool)]
    fn = row(final_norm)

    nb = x_prompt.shape[0]
    shift0 = jnp.zeros((n_rwkv, nb, d), state_shift.dtype)
    wkv0 = jnp.zeros((n_rwkv, nb, n_heads, HEAD_DIM, HEAD_DIM), state_wkv.dtype)
    buf0 = jnp.zeros((n_pool, nb, state_pool.shape[2], c), state_pool.dtype)
    y_p, sh_p, wkv_p, pool_p = _trunk(x_prompt, shift0, wkv0, buf0, 0, rw, pw, vres, fn)
    y_s, sh_s, wkv_s, pool_s = _trunk(x_sample, state_shift, state_wkv, state_pool, PAST_LEN, rw, pw,
                                      vres, fn)
    return (y_p, y_s, sh_p, wkv_p, pool_p, sh_s, wkv_s, pool_s)
```
